```python
import math
import jax, jax.numpy as jnp
from jax import lax
import numpy as np

D_MODEL = 2048
BATCH = 16
SEQ = 2048
DEPTH = 4
DEC_BATCH = 8
DEC_SEQ = 2048
PAST_LEN = 128

HEAD_DIM = 128
ATT_WIDTH = 3 * D_MODEL // 4
POOL_WIDTH = D_MODEL - ATT_WIDTH
N_HEADS = ATT_WIDTH // HEAD_DIM
DILATED_GROUPS = ((128, 1), (512, 4), (2048, 16))
N_GROUPS = len(DILATED_GROUPS)
QKV_WIDTH = N_GROUPS * 3 * ATT_WIDTH
IN_WIDTH = QKV_WIDTH + POOL_WIDTH
POOL_WINDOWS = (2, 4, 8, 16)
POOL_GROUP = POOL_WIDTH // len(POOL_WINDOWS)
D_FF = 5632
N_BUCKETS = 32
MAX_DISTANCE = 1024
EPS = 1e-6
NEG_INF = -1e30

kernel_name = "hybrid_dilated_attn_pool_macaron_encoder"


def rmsnorm(x, g):
    xf = x.astype(jnp.float32)
    y = xf * lax.rsqrt(jnp.mean(xf * xf, axis=-1, keepdims=True) + EPS)
    return (y * g.astype(jnp.float32)).astype(x.dtype)


def swiglu(x, wg, wu, wd):
    return (jax.nn.silu(x @ wg) * (x @ wu)) @ wd


def t5_bucket(rel):
    half = N_BUCKETS // 2
    max_exact = half // 2
    ret = jnp.where(rel > 0, half, 0)
    n = jnp.abs(rel)
    nf = jnp.maximum(n, 1).astype(jnp.float32)
    large = max_exact + (jnp.log(nf / max_exact) / math.log(MAX_DISTANCE / max_exact)
                         * (half - max_exact)).astype(jnp.int32)
    large = jnp.minimum(large, half - 1)
    return ret + jnp.where(n < max_exact, n, large)


def dilated_group_attention(q, k, v, bias_table, dilation, side):
    B, S, H, hd = q.shape
    L = S // dilation
    nblk = -(-L // side)
    Lp = nblk * side

    def to_classes(t):
        t = t.reshape(B, L, dilation, H, hd).transpose(0, 2, 1, 3, 4)
        return jnp.pad(t, ((0, 0), (0, 0), (0, Lp - L), (0, 0), (0, 0)))

    qc, kc, vc = to_classes(q), to_classes(k), to_classes(v)
    qb = qc.reshape(B, dilation, nblk, side, H, hd)

    def windows(t):
        tp = jnp.pad(t, ((0, 0), (0, 0), (side, side), (0, 0), (0, 0)))
        tb = tp.reshape(B, dilation, nblk + 2, side, H, hd)
        return jnp.concatenate([tb[:, :, :-2], tb[:, :, 1:-1], tb[:, :, 2:]], axis=3)

    kw, vw = windows(kc), windows(vc)

    qi = jnp.arange(side)[:, None]
    kj = jnp.arange(3 * side)[None, :]
    delta = kj - side - qi
    kpos = jnp.arange(nblk)[:, None] * side + jnp.arange(3 * side)[None, :] - side
    valid = (kpos >= 0) & (kpos < L)
    mask = valid[:, None, :] & (jnp.abs(delta) <= side)[None]
    bias = bias_table.astype(jnp.float32)[t5_bucket(delta * dilation)].transpose(2, 0, 1)

    s = jnp.einsum('bcnqhd,bcnkhd->bcnhqk', qb, kw,
                   preferred_element_type=jnp.float32) * (HEAD_DIM ** -0.5)
    s = s + bias[None, None, None]
    s = jnp.where(mask[None, None, :, None], s, NEG_INF)
    m = jnp.max(s, axis=-1, keepdims=True)
    p = jnp.exp(s - m)
    l = jnp.sum(p, axis=-1, keepdims=True)
    o = jnp.einsum('bcnhqk,bcnkhd->bcnqhd', p / l, vw.astype(jnp.float32))
    lse = (m + jnp.log(l))[..., 0]

    o = o.reshape(B, dilation, Lp, H, hd)[:, :, :L].transpose(0, 2, 1, 3, 4).reshape(B, S, H, hd)
    lse = lse.transpose(0, 1, 2, 4, 3).reshape(B, dilation, Lp, H)[:, :, :L]
    lse = lse.transpose(0, 2, 1, 3).reshape(B, S, H)
    return o, lse


def multiscale_pool(u, w_pool, pool_scale):
    B, S, _ = u.shape
    uf = u.astype(jnp.float32)
    cs = jnp.pad(jnp.cumsum(uf, axis=1), ((0, 0), (1, 0), (0, 0)))
    pos = jnp.arange(S)
    diffs = []
    for g, w in enumerate(POOL_WINDOWS):
        h = w // 2
        lo = jnp.maximum(pos - h, 0)
        hi = jnp.minimum(pos + h + 1, S)
        seg = cs[:, :, g * POOL_GROUP:(g + 1) * POOL_GROUP]
        mean = (seg[:, hi] - seg[:, lo]) / (hi - lo).astype(jnp.float32)[None, :, None]
        diffs.append(mean - uf[..., g * POOL_GROUP:(g + 1) * POOL_GROUP])
    d = jnp.stack(diffs, axis=2).astype(u.dtype)
    y = jnp.einsum('bsgc,gce->bsge', d, w_pool).reshape(B, S, POOL_WIDTH)
    return y * pool_scale


def token_mixer(h, w_in, w_pool, pool_scale, w_out, rel_bias):
    B, S, _ = h.shape
    z = h @ w_in
    qkv = z[..., :QKV_WIDTH].reshape(B, S, N_GROUPS, 3, N_HEADS, HEAD_DIM)
    u = z[..., QKV_WIDTH:]
    outs, lses = [], []
    for g, (window, dil) in enumerate(DILATED_GROUPS):
        side = window // (2 * dil)
        o, lse = dilated_group_attention(qkv[:, :, g, 0], qkv[:, :, g, 1], qkv[:, :, g, 2],
                                         rel_bias[:, g * N_HEADS:(g + 1) * N_HEADS], dil, side)
        outs.append(o)
        lses.append(lse)
    wts = jax.nn.softmax(jnp.stack(lses), axis=0)
    att = jnp.einsum('gbsh,gbshd->bshd', wts, jnp.stack(outs)).reshape(B, S, ATT_WIDTH).astype(h.dtype)
    pool = multiscale_pool(u, w_pool, pool_scale).astype(h.dtype)
    return jnp.concatenate([att, pool], axis=-1) @ w_out


def trunk(x, norm_g, ffn_gate, ffn_up, ffn_down, w_in, w_pool, pool_scale, w_out, rel_bias, final_g):
    for l in range(DEPTH):
        x = x + 0.5 * swiglu(rmsnorm(x, norm_g[l, 0]), ffn_gate[l, 0], ffn_up[l, 0], ffn_down[l, 0])
        x = x + token_mixer(rmsnorm(x, norm_g[l, 1]), w_in[l], w_pool[l], pool_scale[l], w_out[l], rel_bias)
        x = x + 0.5 * swiglu(rmsnorm(x, norm_g[l, 2]), ffn_gate[l, 1], ffn_up[l, 1], ffn_down[l, 1])
    return rmsnorm(x, final_g)


def setup_inputs(seed: int = 0) -> dict:
    key = jax.random.key(seed)
    ks = jax.random.split(key, 12)
    f32 = jnp.float32
    x_prompt = jax.random.normal(ks[0], (BATCH, SEQ, D_MODEL), f32)
    x_sample = jax.random.normal(ks[1], (DEC_BATCH, DEC_SEQ, D_MODEL), f32)
    norm_g = 1.0 + 0.02 * jax.random.normal(ks[2], (DEPTH, 3, D_MODEL), f32)
    ffn_gate = jax.random.normal(ks[3], (DEPTH, 2, D_MODEL, D_FF), f32) * D_MODEL ** -0.5
    ffn_up = jax.random.normal(ks[4], (DEPTH, 2, D_MODEL, D_FF), f32) * D_MODEL ** -0.5
    ffn_down = jax.random.normal(ks[5], (DEPTH, 2, D_FF, D_MODEL), f32) * D_FF ** -0.5
    w_in = jax.random.normal(ks[6], (DEPTH, D_MODEL, IN_WIDTH), f32) * D_MODEL ** -0.5
    w_pool = jax.random.normal(ks[7], (DEPTH, len(POOL_WINDOWS), POOL_GROUP, POOL_GROUP), f32) * POOL_GROUP ** -0.5
    pool_scale = 1.0 + 0.02 * jax.random.normal(ks[8], (DEPTH, POOL_WIDTH), f32)
    w_out = jax.random.normal(ks[9], (DEPTH, D_MODEL, D_MODEL), f32) * D_MODEL ** -0.5
    rel_bias = 0.3 * jax.random.normal(ks[10], (N_BUCKETS, N_GROUPS * N_HEADS), f32)
    final_g = 1.0 + 0.02 * jax.random.normal(ks[11], (D_MODEL,), f32)
    return {"x_prompt": x_prompt, "x_sample": x_sample, "norm_g": norm_g, "ffn_gate": ffn_gate,
            "ffn_up": ffn_up, "ffn_down": ffn_down, "w_in": w_in, "w_pool": w_pool,
            "pool_scale": pool_scale, "w_out": w_out, "rel_bias": rel_bias, "final_g": final_g}


def reference(x_prompt, x_sample, norm_g, ffn_gate, ffn_up, ffn_down, w_in, w_pool, pool_scale,
              w_out, rel_bias, final_g):
    y_prompt = trunk(x_prompt, norm_g, ffn_gate, ffn_up, ffn_down, w_in, w_pool, pool_scale,
                     w_out, rel_bias, final_g)
    y_sample = trunk(x_sample, norm_g, ffn_gate, ffn_up, ffn_down, w_in, w_pool, pool_scale,
                     w_out, rel_bias, final_g)
    return (y_prompt, y_sample)
```

```python
import functools
import math

import jax
import jax.numpy as jnp
from jax import lax
from jax.experimental import pallas as pl
from jax.experimental.pallas import tpu as pltpu

F32 = jnp.float32
BF16 = jnp.bfloat16

D_MODEL = 2048
HEAD_DIM = 128
N_HEADS = 12
ATT_WIDTH = N_HEADS * HEAD_DIM
POOL_WIDTH = D_MODEL - ATT_WIDTH
POOL_WINDOWS = (2, 4, 8, 16)
POOL_GROUP = POOL_WIDTH // len(POOL_WINDOWS)
POOL_HALO = 8
DILATIONS = (1, 4, 16)
N_GROUPS = len(DILATIONS)
BAND = 64
GROUP_WIDTH = 3 * ATT_WIDTH
QKV_WIDTH = N_GROUPS * GROUP_WIDTH
N_BUCKETS = 32
MAX_DISTANCE = 1024
EPS = 1e-6
NEG_INF = -1e30
LANES = 128
QBLK = 128
KWIN = QBLK + 2 * BAND
V7X_VMEM_BYTES = 64 * 1024 * 1024


def _vmem_limit(nbytes):
    return int(min(nbytes, V7X_VMEM_BYTES - 6 * 1024 * 1024))


def _rms_scale(x):
    return lax.rsqrt(jnp.mean(x * x, axis=-1, keepdims=True) + EPS)


def _ffn_kernel(*refs, final):
    if final:
        x_ref, g_ref, wg_ref, wu_ref, wd_ref, fg_ref, o_ref, xn_ref, acc_ref = refs
    else:
        x_ref, g_ref, wg_ref, wu_ref, wd_ref, o_ref, xn_ref, acc_ref = refs
    j = pl.program_id(1)

    @pl.when(j == 0)
    def _():
        x = x_ref[...]
        xn_ref[...] = (x * _rms_scale(x) * g_ref[...]).astype(BF16)
        acc_ref[...] = jnp.zeros_like(acc_ref)

    xn = xn_ref[...]
    a = jnp.dot(xn, wg_ref[...], preferred_element_type=F32)
    b = jnp.dot(xn, wu_ref[...], preferred_element_type=F32)
    h = (a * jax.nn.sigmoid(a)) * b
    acc_ref[...] += jnp.dot(h.astype(BF16), wd_ref[...], preferred_element_type=F32)

    @pl.when(j == pl.num_programs(1) - 1)
    def _():
        y = x_ref[...] + 0.5 * acc_ref[...]
        if final:
            y = y * _rms_scale(y) * fg_ref[...]
        o_ref[...] = y


def _ffn(x, norm_g, wg, wu, wd, layer, which, norm_idx, final_g=None, *, tm=512, tf=512):
    t, d = x.shape
    f = wg.shape[-1]
    final = final_g is not None
    in_specs = [
        pl.BlockSpec((tm, d), lambda i, j: (i, 0)),
        pl.BlockSpec((None, None, 1, d), lambda i, j: (layer, norm_idx, 0, 0)),
        pl.BlockSpec((None, None, d, tf), lambda i, j: (layer, which, 0, j)),
        pl.BlockSpec((None, None, d, tf), lambda i, j: (layer, which, 0, j)),
        pl.BlockSpec((None, None, tf, d), lambda i, j: (layer, which, j, 0)),
    ]
    args = [x, norm_g, wg, wu, wd]
    if final:
        in_specs.append(pl.BlockSpec((1, d), lambda i, j: (0, 0)))
        args.append(final_g)
    vmem = 4 * tm * d * 4 + tm * d * 4 + tm * d * 2 + 2 * 3 * d * tf * 2 + 4 * tm * tf * 4
    return pl.pallas_call(
        functools.partial(_ffn_kernel, final=final),
        grid=(t // tm, f // tf),
        in_specs=in_specs,
        out_specs=pl.BlockSpec((tm, d), lambda i, j: (i, 0)),
        out_shape=jax.ShapeDtypeStruct((t, d), F32),
        scratch_shapes=[pltpu.VMEM((tm, d), BF16), pltpu.VMEM((tm, d), F32)],
        compiler_params=pltpu.CompilerParams(
            dimension_semantics=("parallel", "arbitrary"),
            vmem_limit_bytes=_vmem_limit(vmem + (8 << 20))),
        name="ffn_final" if final else "ffn",
    )(*args)


def _proj_kernel(*refs, dil, grouped):
    x_refs = refs[:dil]
    g_ref, w_ref, o_ref, xn_ref = refs[dil:]
    j = pl.program_id(1)
    n = x_refs[0].shape[0]

    @pl.when(j == 0)
    def _():
        for c in range(dil):
            xs = x_refs[c][...]
            xn_ref[c * n:(c + 1) * n, :] = (xs * _rms_scale(xs) * g_ref[...]).astype(BF16)

    res = jnp.dot(xn_ref[...], w_ref[...], preferred_element_type=F32)
    if not grouped:
        o_ref[...] = res
    else:
        for kk in range(o_ref.shape[0]):
            for c in range(dil):
                o_ref[kk, c] = res[c * n:(c + 1) * n, kk * LANES:(kk + 1) * LANES].astype(BF16)


def _proj(x, norm_g, w_in, layer, col0, width, dil, batch, seq, *, grouped, tm=512, tn=1536):
    t, d = x.shape
    tn = min(tn, width)
    tps = seq // tm
    kb = tn // LANES
    n = tm // dil
    if grouped:
        out_shape = jax.ShapeDtypeStruct((batch, width // LANES, dil, seq // dil, LANES), BF16)
        out_spec = pl.BlockSpec((None, kb, dil, n, LANES),
                                lambda i, j: (i // tps, j, 0, i % tps, 0))
    else:
        out_shape = jax.ShapeDtypeStruct((t, width), F32)
        out_spec = pl.BlockSpec((tm, tn), lambda i, j: (i, j))
    cb0 = col0 // tn
    x_cls = x.reshape(t // dil, dil * d)
    x_specs = [pl.BlockSpec((n, d), lambda i, j, c=c: (i, c)) for c in range(dil)]
    vmem = 2 * tm * d * 4 + tm * d * 2 + 2 * d * tn * 2 + 2 * tm * tn * 4 + 2 * tm * tn * 4
    return pl.pallas_call(
        functools.partial(_proj_kernel, dil=dil, grouped=grouped),
        grid=(t // tm, width // tn),
        in_specs=x_specs + [
            pl.BlockSpec((None, None, 1, d), lambda i, j: (layer, 1, 0, 0)),
            pl.BlockSpec((None, d, tn), lambda i, j: (layer, 0, cb0 + j)),
        ],
        out_specs=out_spec,
        out_shape=out_shape,
        scratch_shapes=[pltpu.VMEM((tm, d), BF16)],
        compiler_params=pltpu.CompilerParams(
            dimension_semantics=("parallel", "arbitrary"),
            vmem_limit_bytes=_vmem_limit(vmem + (8 << 20))),
        name=f"proj_d{dil}" if grouped else "proj_pool",
    )(*([x_cls] * dil), norm_g, w_in)


def _attn_kernel(q0, k0, v0, q1, k1, v1, q2, k2, v2, bias_ref, o_ref, osc, msc, lsc, *, seq):
    scale = HEAD_DIM ** -0.5
    qkv = ((q0, k0, v0), (q1, k1, v1), (q2, k2, v2))
    for g, dil in enumerate(DILATIONS):
        q_ref, k_ref, v_ref = qkv[g]
        cls_len = seq // dil
        nblk = cls_len // QBLK
        kw = min(KWIN, cls_len)

        def block(idx, carry, g=g, dil=dil, q_ref=q_ref, k_ref=k_ref, v_ref=v_ref,
                  cls_len=cls_len, nblk=nblk, kw=kw):
            c = lax.shift_right_logical(idx, nblk.bit_length() - 1)
            i = lax.bitwise_and(idx, nblk - 1)
            qs = pl.multiple_of(i * QBLK, QBLK)
            ks = pl.multiple_of(jnp.clip(i * QBLK - BAND, 0, cls_len - kw), BAND)
            var = jnp.where(i == 0, 0, jnp.where(i == nblk - 1, 2, 1))
            q = q_ref[c, pl.ds(qs, QBLK), :]
            k = k_ref[c, pl.ds(ks, kw), :]
            v = v_ref[c, pl.ds(ks, kw), :]
            s = lax.dot_general(q, k, (((1,), (1,)), ((), ())), preferred_element_type=F32)
            s = s * scale + bias_ref[g, var, :, 0:kw]
            m = jnp.max(s, axis=-1, keepdims=True)
            p = jnp.exp(s - m)
            l = jnp.sum(p, axis=-1, keepdims=True)
            o = jnp.dot(p.astype(BF16), v, preferred_element_type=F32)
            if dil == 1:
                rows = pl.ds(qs, QBLK)
            else:
                rows = pl.ds(c + dil * qs, QBLK, stride=dil)
            osc[g, rows, :] = o
            msc[g, rows, :] = jnp.broadcast_to(m, (QBLK, LANES))
            lsc[g, rows, :] = jnp.broadcast_to(l, (QBLK, LANES))
            return carry

        lax.fori_loop(0, dil * nblk, block, 0)

    def merge(r, carry):
        rows = pl.ds(pl.multiple_of(r * QBLK, QBLK), QBLK)
        m0, m1, m2 = msc[0, rows, :], msc[1, rows, :], msc[2, rows, :]
        mx = jnp.maximum(jnp.maximum(m0, m1), m2)
        a0, a1, a2 = jnp.exp(m0 - mx), jnp.exp(m1 - mx), jnp.exp(m2 - mx)
        num = a0 * osc[0, rows, :] + a1 * osc[1, rows, :] + a2 * osc[2, rows, :]
        den = a0 * lsc[0, rows, :] + a1 * lsc[1, rows, :] + a2 * lsc[2, rows, :]
        o_ref[rows, :] = (num / den).astype(BF16)
        return carry

    lax.fori_loop(0, seq // QBLK, merge, 0)


def _attention(zs, bias, batch, seq):
    in_specs = []
    args = []
    for g, dil in enumerate(DILATIONS):
        for t in range(3):
            in_specs.append(pl.BlockSpec((None, None, dil, seq // dil, LANES),
                                         lambda b, h, t=t: (b, t * N_HEADS + h, 0, 0, 0)))
            args.append(zs[g])
    in_specs.append(pl.BlockSpec((None, N_GROUPS, 3, QBLK, KWIN), lambda b, h: (h, 0, 0, 0, 0)))
    args.append(bias)
    vmem = 2 * 9 * seq * LANES * 2 + 2 * 9 * QBLK * KWIN * 4 + 2 * seq * LANES * 2 + 9 * seq * LANES * 4
    return pl.pallas_call(
        functools.partial(_attn_kernel, seq=seq),
        grid=(batch, N_HEADS),
        in_specs=in_specs,
        out_specs=pl.BlockSpec((None, seq, LANES), lambda b, h: (b, 0, h)),
        out_shape=jax.ShapeDtypeStruct((batch, seq, ATT_WIDTH), BF16),
        scratch_shapes=[pltpu.VMEM((N_GROUPS, seq, LANES), F32)] * 3,
        compiler_params=pltpu.CompilerParams(
            dimension_semantics=("parallel", "arbitrary"),
            vmem_limit_bytes=_vmem_limit(vmem + (8 << 20))),
        name="attn",
    )(*args)


def _mix_out_kernel(x_ref, att_ref, up_ref, u_ref, un_ref, wp_ref, ps_ref, wo_ref, o_ref, ext_ref,
                    *, seq):
    i = pl.program_id(0)
    tm = x_ref.shape[0]
    pos0 = (i % (seq // tm)) * tm
    ext_ref[0:POOL_HALO, :] = jnp.where(pos0 > 0, up_ref[...], 0.0)
    ext_ref[POOL_HALO:POOL_HALO + tm, :] = u_ref[...]
    ext_ref[POOL_HALO + tm:2 * POOL_HALO + tm, :] = jnp.where(pos0 + tm < seq, un_ref[...], 0.0)
    pos = pos0 + lax.broadcasted_iota(jnp.int32, (tm, 1), 0)

    acc = jnp.dot(att_ref[...], wo_ref[0:ATT_WIDTH, :], preferred_element_type=F32)
    for g, w in enumerate(POOL_WINDOWS):
        h = w // 2
        cols = slice(g * POOL_GROUP, (g + 1) * POOL_GROUP)
        ssum = ext_ref[POOL_HALO - h:POOL_HALO - h + tm, cols]
        for kk in range(-h + 1, h + 1):
            ssum = ssum + ext_ref[POOL_HALO + kk:POOL_HALO + kk + tm, cols]
        cnt = (jnp.minimum(pos + h + 1, seq) - jnp.maximum(pos - h, 0)).astype(F32)
        dg = ssum / cnt - u_ref[:, cols]
        y = jnp.dot(dg.astype(BF16), wp_ref[g], preferred_element_type=F32) * ps_ref[:, cols]
        acc = acc + jnp.dot(y.astype(BF16),
                            wo_ref[ATT_WIDTH + g * POOL_GROUP:ATT_WIDTH + (g + 1) * POOL_GROUP, :],
                            preferred_element_type=F32)
    o_ref[...] = x_ref[...] + acc


def _mix_out(x, att, u, w_pool, pool_scale, w_out, layer, seq, *, tm=512):
    t, d = x.shape
    hb = tm // POOL_HALO
    nhb = t // POOL_HALO
    vmem = 4 * tm * d * 4 + 2 * tm * ATT_WIDTH * 2 + 3 * tm * POOL_WIDTH * 4 + 2 * d * d * 2 + 2 * tm * d * 4
    return pl.pallas_call(
        functools.partial(_mix_out_kernel, seq=seq),
        grid=(t // tm,),
        in_specs=[
            pl.BlockSpec((tm, d), lambda i: (i, 0)),
            pl.BlockSpec((tm, ATT_WIDTH), lambda i: (i, 0)),
            pl.BlockSpec((POOL_HALO, POOL_WIDTH), lambda i: (jnp.maximum(i * hb - 1, 0), 0)),
            pl.BlockSpec((tm, POOL_WIDTH), lambda i: (i, 0)),
            pl.BlockSpec((POOL_HALO, POOL_WIDTH), lambda i: (jnp.minimum((i + 1) * hb, nhb - 1), 0)),
            pl.BlockSpec((None, len(POOL_WINDOWS), POOL_GROUP, POOL_GROUP), lambda i: (layer, 0, 0, 0)),
            pl.BlockSpec((None, 1, POOL_WIDTH), lambda i: (layer, 0, 0)),
            pl.BlockSpec((None, d, d), lambda i: (layer, 0, 0)),
        ],
        out_specs=pl.BlockSpec((tm, d), lambda i: (i, 0)),
        out_shape=jax.ShapeDtypeStruct((t, d), F32),
        scratch_shapes=[pltpu.VMEM((tm + 2 * POOL_HALO, POOL_WIDTH), F32)],
        compiler_params=pltpu.CompilerParams(
            dimension_semantics=("parallel",),
            vmem_limit_bytes=_vmem_limit(vmem + (8 << 20))),
        name="mix_out",
    )(x, att, u, u, u, w_pool, pool_scale, w_out)


def _t5_bucket(rel):
    half = N_BUCKETS // 2
    max_exact = half // 2
    ret = jnp.where(rel > 0, half, 0)
    n = jnp.abs(rel)
    nf = jnp.maximum(n, 1).astype(F32)
    large = max_exact + (jnp.log(nf / max_exact) / math.log(MAX_DISTANCE / max_exact)
                         * (half - max_exact)).astype(jnp.int32)
    large = jnp.minimum(large, half - 1)
    return ret + jnp.where(n < max_exact, n, large)


def _bias_tiles(rel_bias):
    qi = jnp.arange(QBLK)[:, None]
    kj = jnp.arange(KWIN)[None, :]
    tiles = []
    for g, dil in enumerate(DILATIONS):
        table = rel_bias.astype(F32)[:, g * N_HEADS:(g + 1) * N_HEADS]
        per_var = []
        for var in range(3):
            delta = kj - qi - var * BAND
            b = table[_t5_bucket(delta * dil)]
            b = jnp.where((jnp.abs(delta) <= BAND)[..., None], b, NEG_INF)
            per_var.append(b.transpose(2, 0, 1))
        tiles.append(jnp.stack(per_var, axis=1))
    return jnp.stack(tiles, axis=1)


def _trunk(x3, p):
    batch, seq, d = x3.shape
    x = x3.reshape(batch * seq, d)
    depth = p["w_in"].shape[0]
    for l in range(depth):
        x = _ffn(x, p["norm_g"], p["ffn_gate"], p["ffn_up"], p["ffn_down"], l, 0, 0)
        zs = [_proj(x, p["norm_g"], p["w_in"], l, g * GROUP_WIDTH, GROUP_WIDTH, dil, batch, seq,
                    grouped=True) for g, dil in enumerate(DILATIONS)]
        u = _proj(x, p["norm_g"], p["w_in"], l, QKV_WIDTH, POOL_WIDTH, 1, batch, seq, grouped=False)
        att = _attention(zs, p["bias"], batch, seq).reshape(batch * seq, ATT_WIDTH)
        x = _mix_out(x, att, u, p["w_pool"], p["pool_scale"], p["w_out"], l, seq)
        x = _ffn(x, p["norm_g"], p["ffn_gate"], p["ffn_up"], p["ffn_down"], l, 1, 2,
                 p["final_g"] if l == depth - 1 else None)
    return x.reshape(batch, seq, d)


def kernel(x_prompt, x_sample, norm_g, ffn_gate, ffn_up, ffn_down, w_in, w_pool, pool_scale,
           w_out, rel_bias, final_g):
    depth = w_in.shape[0]
    p = {
        "norm_g": norm_g.reshape(depth, 3, 1, D_MODEL),
        "ffn_gate": ffn_gate.astype(BF16),
        "ffn_up": ffn_up.astype(BF16),
        "ffn_down": ffn_down.astype(BF16),
        "w_in": w_in.astype(BF16),
        "w_pool": w_pool.astype(BF16),
        "pool_scale": pool_scale.reshape(depth, 1, POOL_WIDTH),
        "w_out": w_out.astype(BF16),
        "bias": _bias_tiles(rel_bias),
        "final_g": final_g.reshape(1, D_MODEL),
    }
    return (_trunk(x_prompt, p), _trunk(x_sample, p))
```

```python
import functools
import math

import jax
import jax.numpy as jnp
from jax import lax
from jax.experimental import pallas as pl
from jax.experimental.pallas import tpu as pltpu

F32 = jnp.float32
BF16 = jnp.bfloat16

D_MODEL = 2048
HEAD_DIM = 128
N_HEADS = 12
ATT_WIDTH = N_HEADS * HEAD_DIM
POOL_WIDTH = D_MODEL - ATT_WIDTH
POOL_WINDOWS = (2, 4, 8, 16)
POOL_GROUP = POOL_WIDTH // len(POOL_WINDOWS)
POOL_HALO = 8
N_GROUPS = 3
CLS = 16
BAND = 64
GROUP_WIDTH = 3 * ATT_WIDTH
QKV_WIDTH = N_GROUPS * GROUP_WIDTH
N_BUCKETS = 32
MAX_DISTANCE = 1024
EPS = 1e-6
NEG_INF = -1e30
LANES = 128
QBLK = 128
KWIN = QBLK + 2 * BAND
TOKEN_TILE = 512
V7X_VMEM_BYTES = 64 * 1024 * 1024


def _vmem_limit(nbytes):
    return int(min(nbytes, V7X_VMEM_BYTES - 6 * 1024 * 1024))


def _rms_scale(x):
    return lax.rsqrt(jnp.mean(x * x, axis=-1, keepdims=True) + EPS)


def _ffn_kernel(*refs, final):
    if final:
        x_ref, g_ref, wg_ref, wu_ref, wd_ref, fg_ref, o_ref, xn_ref, acc_ref = refs
    else:
        x_ref, g_ref, wg_ref, wu_ref, wd_ref, o_ref, xn_ref, acc_ref = refs
    j = pl.program_id(1)

    @pl.when(j == 0)
    def _():
        x = x_ref[...]
        xn_ref[...] = (x * _rms_scale(x) * g_ref[...]).astype(BF16)
        acc_ref[...] = jnp.zeros_like(acc_ref)

    xn = xn_ref[...]
    a = jnp.dot(xn, wg_ref[...], preferred_element_type=F32)
    b = jnp.dot(xn, wu_ref[...], preferred_element_type=F32)
    h = (a * jax.nn.sigmoid(a)) * b
    acc_ref[...] += jnp.dot(h.astype(BF16), wd_ref[...], preferred_element_type=F32)

    @pl.when(j == pl.num_programs(1) - 1)
    def _():
        y = x_ref[...] + 0.5 * acc_ref[...]
        if final:
            y = y * _rms_scale(y) * fg_ref[...]
        o_ref[...] = y


def _ffn(x, norm_g, wg, wu, wd, layer, which, norm_idx, final_g=None, *, tm=TOKEN_TILE, tf=512):
    t, d = x.shape
    f = wg.shape[-1]
    final = final_g is not None
    in_specs = [
        pl.BlockSpec((tm, d), lambda i, j: (i, 0)),
        pl.BlockSpec((None, None, 1, d), lambda i, j: (layer, norm_idx, 0, 0)),
        pl.BlockSpec((None, None, d, tf), lambda i, j: (layer, which, 0, j)),
        pl.BlockSpec((None, None, d, tf), lambda i, j: (layer, which, 0, j)),
        pl.BlockSpec((None, None, tf, d), lambda i, j: (layer, which, j, 0)),
    ]
    args = [x, norm_g, wg, wu, wd]
    if final:
        in_specs.append(pl.BlockSpec((1, d), lambda i, j: (0, 0)))
        args.append(final_g)
    vmem = 4 * tm * d * 4 + tm * d * 4 + tm * d * 2 + 2 * 3 * d * tf * 2 + 4 * tm * tf * 4
    return pl.pallas_call(
        functools.partial(_ffn_kernel, final=final),
        grid=(t // tm, f // tf),
        in_specs=in_specs,
        out_specs=pl.BlockSpec((tm, d), lambda i, j: (i, 0)),
        out_shape=jax.ShapeDtypeStruct((t, d), F32),
        scratch_shapes=[pltpu.VMEM((tm, d), BF16), pltpu.VMEM((tm, d), F32)],
        compiler_params=pltpu.CompilerParams(
            dimension_semantics=("parallel", "arbitrary"),
            vmem_limit_bytes=_vmem_limit(vmem + (8 << 20))),
        name="ffn_final" if final else "ffn",
    )(*args)


def _proj_nat_kernel(x_ref, g_ref, w_ref, wu_ref, z_ref, u_ref, xn_ref):
    j = pl.program_id(1)

    @pl.when(j == 0)
    def _():
        x = x_ref[...]
        xn_ref[...] = (x * _rms_scale(x) * g_ref[...]).astype(BF16)
        u_ref[...] = jnp.dot(xn_ref[...], wu_ref[...], preferred_element_type=F32)

    res = jnp.dot(xn_ref[...], w_ref[...], preferred_element_type=F32)
    for kk in range(z_ref.shape[0]):
        z_ref[kk] = res[:, kk * LANES:(kk + 1) * LANES].astype(BF16)


def _proj_cls_kernel(x_ref, g_ref, perm_ref, w_ref, z_ref, xn_ref):
    j = pl.program_id(1)
    n = z_ref.shape[2]

    @pl.when(j == 0)
    def _():
        x = x_ref[...]
        xn = (x * _rms_scale(x) * g_ref[...]).astype(BF16)
        xn_ref[...] = jnp.dot(perm_ref[...], xn, preferred_element_type=F32).astype(BF16)

    res = jnp.dot(xn_ref[...], w_ref[...], preferred_element_type=F32)
    for kk in range(z_ref.shape[0]):
        for c in range(CLS):
            z_ref[kk, c] = res[c * n:(c + 1) * n, kk * LANES:(kk + 1) * LANES].astype(BF16)


def _proj_vmem(tm, d, tn):
    return 2 * tm * d * 4 + tm * d * 2 + 2 * d * tn * 2 + 2 * tm * tn * 2 + 2 * tm * tn * 4


def _proj_nat(x, norm_g, w_in, layer, batch, seq, *, tm=TOKEN_TILE, tn=1536):
    t, d = x.shape
    tps = seq // tm
    ucb = QKV_WIDTH // POOL_WIDTH
    return pl.pallas_call(
        _proj_nat_kernel,
        grid=(t // tm, GROUP_WIDTH // tn),
        in_specs=[
            pl.BlockSpec((tm, d), lambda i, j: (i, 0)),
            pl.BlockSpec((None, None, 1, d), lambda i, j: (layer, 1, 0, 0)),
            pl.BlockSpec((None, d, tn), lambda i, j: (layer, 0, j)),
            pl.BlockSpec((None, d, POOL_WIDTH), lambda i, j: (layer, 0, ucb)),
        ],
        out_specs=[
            pl.BlockSpec((None, tn // LANES, tm, LANES), lambda i, j: (i // tps, j, i % tps, 0)),
            pl.BlockSpec((tm, POOL_WIDTH), lambda i, j: (i, 0)),
        ],
        out_shape=[
            jax.ShapeDtypeStruct((batch, GROUP_WIDTH // LANES, seq, LANES), BF16),
            jax.ShapeDtypeStruct((t, POOL_WIDTH), F32),
        ],
        scratch_shapes=[pltpu.VMEM((tm, d), BF16)],
        compiler_params=pltpu.CompilerParams(
            dimension_semantics=("parallel", "arbitrary"),
            vmem_limit_bytes=_vmem_limit(_proj_vmem(tm, d, tn) + 4 * tm * POOL_WIDTH * 4 + (8 << 20))),
        name="proj_nat",
    )(x, norm_g, w_in, w_in)


def _class_major_perm(tm):
    r = jnp.arange(tm)
    src = (r % (tm // CLS)) * CLS + r // (tm // CLS)
    return (src[:, None] == r[None, :]).astype(BF16)


def _proj_cls(x, norm_g, w_in, perm, layer, batch, seq, *, tm=TOKEN_TILE, tn=1536):
    t, d = x.shape
    tps = seq // tm
    cb0 = GROUP_WIDTH // tn
    width = 2 * GROUP_WIDTH
    return pl.pallas_call(
        _proj_cls_kernel,
        grid=(t // tm, width // tn),
        in_specs=[
            pl.BlockSpec((tm, d), lambda i, j: (i, 0)),
            pl.BlockSpec((None, None, 1, d), lambda i, j: (layer, 1, 0, 0)),
            pl.BlockSpec((tm, tm), lambda i, j: (0, 0)),
            pl.BlockSpec((None, d, tn), lambda i, j: (layer, 0, cb0 + j)),
        ],
        out_specs=pl.BlockSpec((None, tn // LANES, CLS, tm // CLS, LANES),
                               lambda i, j: (i // tps, j, 0, i % tps, 0)),
        out_shape=jax.ShapeDtypeStruct((batch, width // LANES, CLS, seq // CLS, LANES), BF16),
        scratch_shapes=[pltpu.VMEM((tm, d), BF16)],
        compiler_params=pltpu.CompilerParams(
            dimension_semantics=("parallel", "arbitrary"),
            vmem_limit_bytes=_vmem_limit(_proj_vmem(tm, d, tn) + 2 * tm * tm * 2 + (8 << 20))),
        name="proj_cls",
    )(x, norm_g, perm, w_in)


ATTN_UNROLL = (8, 2, 8)


def _scores(q, k, bias):
    s = lax.dot_general(q, k, (((1,), (1,)), ((), ())), preferred_element_type=F32)
    return s * (HEAD_DIM ** -0.5) + bias


def _softmax(s):
    m = jnp.max(s, axis=-1, keepdims=True)
    p = jnp.exp(s - m)
    l = jnp.sum(p, axis=-1, keepdims=True)
    return p.astype(BF16), m, l


def _attn_kernel(q0, k0, v0, q1, k1, v1, q2, k2, v2, b01_ref, b2_ref, o_ref, osc, lsc, *, seq):
    u2, u1, u0 = ATTN_UNROLL
    cls_len = seq // CLS
    sub = QBLK // 4
    ksub = KWIN // 4

    def g2_body(it, carry):
        cs = [it * u2 + u for u in range(u2)]
        ss = [_scores(q2[c], k2[c], b2_ref[...]) for c in cs]
        pml = [_softmax(s) for s in ss]
        outs = [jnp.dot(p, v2[c], preferred_element_type=F32) for (p, _, _), c in zip(pml, cs)]
        for c, o, (_, m, l) in zip(cs, outs, pml):
            rows = pl.ds(c, cls_len, stride=CLS)
            osc[1, rows, :] = o * (1.0 / l)
            lsc[1, rows, :] = jnp.broadcast_to(m + jnp.log(l), (QBLK, LANES))
        return carry

    lax.fori_loop(0, CLS // u2, g2_body, 0)

    nblk1 = (seq // 4) // QBLK
    kss1 = [min(max(sub * i - BAND // 4, 0), cls_len - ksub) for i in range(nblk1)]
    var1 = [0 if i == 0 else (2 if i == nblk1 - 1 else 1) for i in range(nblk1)]

    def g1_body(it, carry):
        blocks = [(it * u1 + uu, i) for uu in range(u1) for i in range(nblk1)]
        ss = []
        for c4, i in blocks:
            q = jnp.concatenate([q1[4 * a + c4, sub * i:sub * (i + 1), :] for a in range(4)], axis=0)
            k = jnp.concatenate([k1[4 * a + c4, kss1[i]:kss1[i] + ksub, :] for a in range(4)], axis=0)
            ss.append(_scores(q, k, b01_ref[1, var1[i]]))
        pml = [_softmax(s) for s in ss]
        outs = []
        for (c4, i), (p, _, _) in zip(blocks, pml):
            v = jnp.concatenate([v1[4 * a + c4, kss1[i]:kss1[i] + ksub, :] for a in range(4)], axis=0)
            outs.append(jnp.dot(p, v, preferred_element_type=F32))
        for (c4, i), o, (_, m, l) in zip(blocks, outs, pml):
            o = o * (1.0 / l)
            lse = jnp.broadcast_to(m + jnp.log(l), (QBLK, LANES))
            for a in range(4):
                rows = pl.ds(CLS * sub * i + 4 * a + c4, sub, stride=CLS)
                osc[0, rows, :] = o[sub * a:sub * (a + 1)]
                lsc[0, rows, :] = lse[sub * a:sub * (a + 1)]
        return carry

    lax.fori_loop(0, 4 // u1, g1_body, 0)

    nblk0 = seq // QBLK

    def g0_body(it, carry):
        idx = [it * u0 + u for u in range(u0)]
        qss = [pl.multiple_of(i * QBLK, QBLK) for i in idx]
        kss = [pl.multiple_of(jnp.clip(i * QBLK - BAND, 0, seq - KWIN), BAND) for i in idx]
        var = [jnp.where(i == 0, 0, jnp.where(i == nblk0 - 1, 2, 1)) for i in idx]
        ss = [_scores(q0[pl.ds(qs, QBLK), :], k0[pl.ds(ks, KWIN), :], b01_ref[0, vr])
              for qs, ks, vr in zip(qss, kss, var)]
        pml = [_softmax(s) for s in ss]
        outs = [jnp.dot(p, v0[pl.ds(ks, KWIN), :], preferred_element_type=F32)
                for (p, _, _), ks in zip(pml, kss)]
        for qs, o0, (_, m0, l0) in zip(qss, outs, pml):
            rows = pl.ds(qs, QBLK)
            lse1 = lsc[0, rows, :]
            lse2 = lsc[1, rows, :]
            mx = jnp.maximum(jnp.maximum(lse1, lse2), m0)
            w0 = jnp.exp(m0 - mx)
            w1 = jnp.exp(lse1 - mx)
            w2 = jnp.exp(lse2 - mx)
            num = w0 * o0 + w1 * osc[0, rows, :] + w2 * osc[1, rows, :]
            den = w0 * l0 + w1 + w2
            o_ref[rows, :] = (num / den).astype(BF16)
        return carry

    lax.fori_loop(0, nblk0 // u0, g0_body, 0)


def _attention(z0, z12, b01, b2, batch, seq):
    in_specs = []
    args = []
    for t in range(3):
        in_specs.append(pl.BlockSpec((None, None, seq, LANES),
                                     lambda h, b, t=t: (b, t * N_HEADS + h, 0, 0)))
        args.append(z0)
    for g in range(2):
        for t in range(3):
            in_specs.append(pl.BlockSpec((None, None, CLS, seq // CLS, LANES),
                                         lambda h, b, t=t, g=g: (b, (3 * g + t) * N_HEADS + h, 0, 0, 0)))
            args.append(z12)
    in_specs.append(pl.BlockSpec((None, 2, 3, QBLK, KWIN), lambda h, b: (h, 0, 0, 0, 0)))
    in_specs.append(pl.BlockSpec((None, QBLK, QBLK), lambda h, b: (h, 0, 0)))
    args += [b01, b2]
    vmem = (2 * 9 * seq * LANES * 2 + 2 * 7 * QBLK * KWIN * 4 + 2 * seq * LANES * 2
            + 4 * seq * LANES * 4)
    return pl.pallas_call(
        functools.partial(_attn_kernel, seq=seq),
        grid=(N_HEADS, batch),
        in_specs=in_specs,
        out_specs=pl.BlockSpec((None, seq, LANES), lambda h, b: (b, 0, h)),
        out_shape=jax.ShapeDtypeStruct((batch, seq, ATT_WIDTH), BF16),
        scratch_shapes=[pltpu.VMEM((2, seq, LANES), F32)] * 2,
        compiler_params=pltpu.CompilerParams(
            dimension_semantics=("parallel", "arbitrary"),
            vmem_limit_bytes=_vmem_limit(vmem + (16 << 20))),
        name="attn",
    )(*args)


def _mix_out_kernel(x_ref, att_ref, up_ref, u_ref, un_ref, wp_ref, ps_ref, wo_ref, o_ref, ext_ref,
                    *, seq):
    i = pl.program_id(0)
    tm = x_ref.shape[0]
    pos0 = (i % (seq // tm)) * tm
    ext_ref[0:POOL_HALO, :] = jnp.where(pos0 > 0, up_ref[...], 0.0)
    ext_ref[POOL_HALO:POOL_HALO + tm, :] = u_ref[...]
    ext_ref[POOL_HALO + tm:2 * POOL_HALO + tm, :] = jnp.where(pos0 + tm < seq, un_ref[...], 0.0)
    pos = pos0 + lax.broadcasted_iota(jnp.int32, (tm, 1), 0)

    acc = jnp.dot(att_ref[...], wo_ref[0:ATT_WIDTH, :], preferred_element_type=F32)
    for g, w in enumerate(POOL_WINDOWS):
        h = w // 2
        cols = slice(g * POOL_GROUP, (g + 1) * POOL_GROUP)
        ssum = ext_ref[POOL_HALO - h:POOL_HALO - h + tm, cols]
        for kk in range(-h + 1, h + 1):
            ssum = ssum + ext_ref[POOL_HALO + kk:POOL_HALO + kk + tm, cols]
        cnt = (jnp.minimum(pos + h + 1, seq) - jnp.maximum(pos - h, 0)).astype(F32)
        dg = ssum / cnt - u_ref[:, cols]
        y = jnp.dot(dg.astype(BF16), wp_ref[g], preferred_element_type=F32) * ps_ref[:, cols]
        acc = acc + jnp.dot(y.astype(BF16),
                            wo_ref[ATT_WIDTH + g * POOL_GROUP:ATT_WIDTH + (g + 1) * POOL_GROUP, :],
                            preferred_element_type=F32)
    o_ref[...] = x_ref[...] + acc


def _mix_out(x, att, u, w_pool, pool_scale, w_out, layer, seq, *, tm=TOKEN_TILE):
    t, d = x.shape
    hb = tm // POOL_HALO
    nhb = t // POOL_HALO
    vmem = 4 * tm * d * 4 + 2 * tm * ATT_WIDTH * 2 + 3 * tm * POOL_WIDTH * 4 + 2 * d * d * 2 + 2 * tm * d * 4
    return pl.pallas_call(
        functools.partial(_mix_out_kernel, seq=seq),
        grid=(t // tm,),
        in_specs=[
            pl.BlockSpec((tm, d), lambda i: (i, 0)),
            pl.BlockSpec((tm, ATT_WIDTH), lambda i: (i, 0)),
            pl.BlockSpec((POOL_HALO, POOL_WIDTH), lambda i: (jnp.maximum(i * hb - 1, 0), 0)),
            pl.BlockSpec((tm, POOL_WIDTH), lambda i: (i, 0)),
            pl.BlockSpec((POOL_HALO, POOL_WIDTH), lambda i: (jnp.minimum((i + 1) * hb, nhb - 1), 0)),
            pl.BlockSpec((None, len(POOL_WINDOWS), POOL_GROUP, POOL_GROUP), lambda i: (layer, 0, 0, 0)),
            pl.BlockSpec((None, 1, POOL_WIDTH), lambda i: (layer, 0, 0)),
            pl.BlockSpec((None, d, d), lambda i: (layer, 0, 0)),
        ],
        out_specs=pl.BlockSpec((tm, d), lambda i: (i, 0)),
        out_shape=jax.ShapeDtypeStruct((t, d), F32),
        scratch_shapes=[pltpu.VMEM((tm + 2 * POOL_HALO, POOL_WIDTH), F32)],
        compiler_params=pltpu.CompilerParams(
            dimension_semantics=("parallel",),
            vmem_limit_bytes=_vmem_limit(vmem + (8 << 20))),
        name="mix_out",
    )(x, att, u, u, u, w_pool, pool_scale, w_out)


def _t5_bucket(rel):
    half = N_BUCKETS // 2
    max_exact = half // 2
    ret = jnp.where(rel > 0, half, 0)
    n = jnp.abs(rel)
    nf = jnp.maximum(n, 1).astype(F32)
    large = max_exact + (jnp.log(nf / max_exact) / math.log(MAX_DISTANCE / max_exact)
                         * (half - max_exact)).astype(jnp.int32)
    large = jnp.minimum(large, half - 1)
    return ret + jnp.where(n < max_exact, n, large)


def _bias_lookup(table, delta, dil):
    onehot = (_t5_bucket(delta * dil)[..., None] == jnp.arange(N_BUCKETS)).astype(F32)
    b = jnp.einsum("...k,kh->h...", onehot, table, precision=lax.Precision.HIGHEST)
    return jnp.where(jnp.abs(delta) <= BAND, b, NEG_INF)


def _bias_tiles(rel_bias):
    rb = rel_bias.astype(F32)
    qi = jnp.arange(QBLK)[:, None]
    kj = jnp.arange(KWIN)[None, :]
    q1 = 4 * (qi % (QBLK // 4)) + qi // (QBLK // 4)
    k1 = 4 * (kj % (KWIN // 4)) + kj // (KWIN // 4)
    g0 = jnp.stack([_bias_lookup(rb[:, 0:N_HEADS], kj - qi - v * BAND, 1) for v in range(3)], axis=1)
    g1 = jnp.stack([_bias_lookup(rb[:, N_HEADS:2 * N_HEADS], k1 - q1 - v * BAND, 4)
                    for v in range(3)], axis=1)
    b2 = _bias_lookup(rb[:, 2 * N_HEADS:3 * N_HEADS], jnp.arange(QBLK)[None, :] - qi, 16)
    return jnp.stack([g0, g1], axis=1), b2


def _trunk(x3, p):
    batch, seq, d = x3.shape
    x = x3.reshape(batch * seq, d)
    depth = p["w_in"].shape[0]
    for l in range(depth):
        x = _ffn(x, p["norm_g"], p["ffn_gate"], p["ffn_up"], p["ffn_down"], l, 0, 0)
        z0, u = _proj_nat(x, p["norm_g"], p["w_in"], l, batch, seq)
        z12 = _proj_cls(x, p["norm_g"], p["w_in"], p["perm"], l, batch, seq)
        att = _attention(z0, z12, p["b01"], p["b2"], batch, seq).reshape(batch * seq, ATT_WIDTH)
        x = _mix_out(x, att, u, p["w_pool"], p["pool_scale"], p["w_out"], l, seq)
        x = _ffn(x, p["norm_g"], p["ffn_gate"], p["ffn_up"], p["ffn_down"], l, 1, 2,
                 p["final_g"] if l == depth - 1 else None)
    return x.reshape(batch, seq, d)


def kernel(x_prompt, x_sample, norm_g, ffn_gate, ffn_up, ffn_down, w_in, w_pool, pool_scale,
           w_out, rel_bias, final_g):
    depth = w_in.shape[0]
    b01, b2 = _bias_tiles(rel_bias)
    p = {
        "norm_g": norm_g.reshape(depth, 3, 1, D_MODEL),
        "ffn_gate": ffn_gate.astype(BF16),
        "ffn_up": ffn_up.astype(BF16),
        "ffn_down": ffn_down.astype(BF16),
        "w_in": w_in.astype(BF16),
        "w_pool": w_pool.astype(BF16),
        "pool_scale": pool_scale.reshape(depth, 1, POOL_WIDTH),
        "w_out": w_out.astype(BF16),
        "b01": b01,
        "b2": b2,
        "perm": _class_major_perm(TOKEN_TILE),
        "final_g": final_g.reshape(1, D_MODEL),
    }
    return (_trunk(x_prompt, p), _trunk(x_sample, p))
```

```python
import functools
import math

import jax
import jax.numpy as jnp
from jax import lax
from jax.experimental import pallas as pl
from jax.experimental.pallas import tpu as pltpu

F32 = jnp.float32
BF16 = jnp.bfloat16

D_MODEL = 2048
HEAD_DIM = 128
N_HEADS = 12
ATT_WIDTH = N_HEADS * HEAD_DIM
POOL_WIDTH = D_MODEL - ATT_WIDTH
POOL_WINDOWS = (2, 4, 8, 16)
POOL_GROUP = POOL_WIDTH // len(POOL_WINDOWS)
POOL_HALO = 8
POOL_ROWS = 128
N_GROUPS = 3
CLS = 16
BAND = 64
GROUP_WIDTH = 3 * ATT_WIDTH
QKV_WIDTH = N_GROUPS * GROUP_WIDTH
N_BUCKETS = 32
MAX_DISTANCE = 1024
EPS = 1e-6
NEG_INF = -1e30
LANES = 128
QBLK = 128
KWIN = QBLK + 2 * BAND
TOKEN_TILE = 1024
ROW_CHUNK = 512
V7X_VMEM_BYTES = 64 * 1024 * 1024


def _vmem_limit(nbytes):
    return int(min(nbytes, V7X_VMEM_BYTES - 6 * 1024 * 1024))


def _rms_scale(x):
    return lax.rsqrt(jnp.mean(x * x, axis=-1, keepdims=True) + EPS)


def _ffn_kernel(*refs, final):
    if final:
        x_ref, g_ref, wg_ref, wu_ref, wd_ref, fg_ref, o_ref, xn_ref = refs
    else:
        x_ref, g_ref, wg_ref, wu_ref, wd_ref, o_ref, xn_ref = refs
    j = pl.program_id(1)

    @pl.when(j == 0)
    def _():
        x = x_ref[...]
        xn_ref[...] = (x * _rms_scale(x) * g_ref[...]).astype(BF16)
        o_ref[...] = x

    for r in range(x_ref.shape[0] // ROW_CHUNK):
        rows = slice(r * ROW_CHUNK, (r + 1) * ROW_CHUNK)
        xn = xn_ref[rows, :]
        a = jnp.dot(xn, wg_ref[...], preferred_element_type=F32)
        b = jnp.dot(xn, wu_ref[...], preferred_element_type=F32)
        h = (a * jax.nn.sigmoid(a)) * (0.5 * b)
        o_ref[rows, :] += jnp.dot(h.astype(BF16), wd_ref[...], preferred_element_type=F32)

    if final:
        @pl.when(j == pl.num_programs(1) - 1)
        def _():
            y = o_ref[...]
            o_ref[...] = y * _rms_scale(y) * fg_ref[...]


def _ffn(x, norm_g, wg, wu, wd, layer, which, norm_idx, final_g=None, *, tm=TOKEN_TILE, tf=512):
    t, d = x.shape
    f = wg.shape[-1]
    final = final_g is not None
    in_specs = [
        pl.BlockSpec((tm, d), lambda i, j: (i, 0)),
        pl.BlockSpec((None, None, 1, d), lambda i, j: (layer, norm_idx, 0, 0)),
        pl.BlockSpec((None, None, d, tf), lambda i, j: (layer, which, 0, j)),
        pl.BlockSpec((None, None, d, tf), lambda i, j: (layer, which, 0, j)),
        pl.BlockSpec((None, None, tf, d), lambda i, j: (layer, which, j, 0)),
    ]
    args = [x, norm_g, wg, wu, wd]
    if final:
        in_specs.append(pl.BlockSpec((1, d), lambda i, j: (0, 0)))
        args.append(final_g)
    vmem = 4 * tm * d * 4 + tm * d * 2 + 2 * 3 * d * tf * 2 + 4 * ROW_CHUNK * tf * 4
    return pl.pallas_call(
        functools.partial(_ffn_kernel, final=final),
        grid=(t // tm, f // tf),
        in_specs=in_specs,
        out_specs=pl.BlockSpec((tm, d), lambda i, j: (i, 0)),
        out_shape=jax.ShapeDtypeStruct((t, d), F32),
        scratch_shapes=[pltpu.VMEM((tm, d), BF16)],
        compiler_params=pltpu.CompilerParams(
            dimension_semantics=("parallel", "arbitrary"),
            vmem_limit_bytes=_vmem_limit(vmem + (8 << 20))),
        name="ffn_final" if final else "ffn",
    )(*args)


def _proj_nat_kernel(x_ref, g_ref, w_ref, wu_ref, z_ref, u_ref, xn_ref):
    j = pl.program_id(1)
    chunks = [slice(r * ROW_CHUNK, (r + 1) * ROW_CHUNK) for r in range(x_ref.shape[0] // ROW_CHUNK)]

    @pl.when(j == 0)
    def _():
        for rows in chunks:
            x = x_ref[rows, :]
            xn_ref[rows, :] = (x * _rms_scale(x) * g_ref[...]).astype(BF16)
            u_ref[rows, :] = jnp.dot(xn_ref[rows, :], wu_ref[...], preferred_element_type=F32)

    for rows in chunks:
        res = jnp.dot(xn_ref[rows, :], w_ref[...], preferred_element_type=F32)
        for kk in range(z_ref.shape[0]):
            z_ref[kk, rows, :] = res[:, kk * LANES:(kk + 1) * LANES].astype(BF16)


def _proj_cls_kernel(x_ref, g_ref, perm_ref, w_ref, z_ref, xn_ref):
    j = pl.program_id(1)
    n = ROW_CHUNK // CLS
    chunks = [slice(r * ROW_CHUNK, (r + 1) * ROW_CHUNK) for r in range(x_ref.shape[0] // ROW_CHUNK)]

    @pl.when(j == 0)
    def _():
        for rows in chunks:
            x = x_ref[rows, :]
            xn = (x * _rms_scale(x) * g_ref[...]).astype(BF16)
            xn_ref[rows, :] = jnp.dot(perm_ref[...], xn, preferred_element_type=F32).astype(BF16)

    for r, rows in enumerate(chunks):
        res = jnp.dot(xn_ref[rows, :], w_ref[...], preferred_element_type=F32)
        for kk in range(z_ref.shape[0]):
            for c in range(CLS):
                z_ref[kk, c, r * n:(r + 1) * n, :] = (
                    res[c * n:(c + 1) * n, kk * LANES:(kk + 1) * LANES].astype(BF16))


def _proj_vmem(tm, d, tn):
    return 2 * tm * d * 4 + tm * d * 2 + 2 * d * tn * 2 + 2 * tm * tn * 2 + 2 * ROW_CHUNK * tn * 4


def _proj_nat(x, norm_g, w_in, layer, batch, seq, *, tm=TOKEN_TILE, tn=1536):
    t, d = x.shape
    tps = seq // tm
    ucb = QKV_WIDTH // POOL_WIDTH
    return pl.pallas_call(
        _proj_nat_kernel,
        grid=(t // tm, GROUP_WIDTH // tn),
        in_specs=[
            pl.BlockSpec((tm, d), lambda i, j: (i, 0)),
            pl.BlockSpec((None, None, 1, d), lambda i, j: (layer, 1, 0, 0)),
            pl.BlockSpec((None, d, tn), lambda i, j: (layer, 0, j)),
            pl.BlockSpec((None, d, POOL_WIDTH), lambda i, j: (layer, 0, ucb)),
        ],
        out_specs=[
            pl.BlockSpec((None, tn // LANES, tm, LANES), lambda i, j: (i // tps, j, i % tps, 0)),
            pl.BlockSpec((tm, POOL_WIDTH), lambda i, j: (i, 0)),
        ],
        out_shape=[
            jax.ShapeDtypeStruct((batch, GROUP_WIDTH // LANES, seq, LANES), BF16),
            jax.ShapeDtypeStruct((t, POOL_WIDTH), F32),
        ],
        scratch_shapes=[pltpu.VMEM((tm, d), BF16)],
        compiler_params=pltpu.CompilerParams(
            dimension_semantics=("parallel", "arbitrary"),
            vmem_limit_bytes=_vmem_limit(_proj_vmem(tm, d, tn) + 4 * tm * POOL_WIDTH * 4 + (8 << 20))),
        name="proj_nat",
    )(x, norm_g, w_in, w_in)


def _class_major_perm(tm):
    r = jnp.arange(tm)
    src = (r % (tm // CLS)) * CLS + r // (tm // CLS)
    return (src[:, None] == r[None, :]).astype(BF16)


def _proj_cls(x, norm_g, w_in, perm, layer, batch, seq, *, tm=TOKEN_TILE, tn=1536):
    t, d = x.shape
    tps = seq // tm
    cb0 = GROUP_WIDTH // tn
    width = 2 * GROUP_WIDTH
    return pl.pallas_call(
        _proj_cls_kernel,
        grid=(t // tm, width // tn),
        in_specs=[
            pl.BlockSpec((tm, d), lambda i, j: (i, 0)),
            pl.BlockSpec((None, None, 1, d), lambda i, j: (layer, 1, 0, 0)),
            pl.BlockSpec((ROW_CHUNK, ROW_CHUNK), lambda i, j: (0, 0)),
            pl.BlockSpec((None, d, tn), lambda i, j: (layer, 0, cb0 + j)),
        ],
        out_specs=pl.BlockSpec((None, tn // LANES, CLS, tm // CLS, LANES),
                               lambda i, j: (i // tps, j, 0, i % tps, 0)),
        out_shape=jax.ShapeDtypeStruct((batch, width // LANES, CLS, seq // CLS, LANES), BF16),
        scratch_shapes=[pltpu.VMEM((tm, d), BF16)],
        compiler_params=pltpu.CompilerParams(
            dimension_semantics=("parallel", "arbitrary"),
            vmem_limit_bytes=_vmem_limit(_proj_vmem(tm, d, tn) + 2 * ROW_CHUNK * ROW_CHUNK * 2 + (8 << 20))),
        name="proj_cls",
    )(x, norm_g, perm, w_in)


ATTN_UNROLL = (8, 2, 8)


def _scores(q, k, bias):
    s = lax.dot_general(q, k, (((1,), (1,)), ((), ())), preferred_element_type=F32)
    return s * (HEAD_DIM ** -0.5) + bias


def _softmax(s):
    m = jnp.max(s, axis=-1, keepdims=True)
    p = jnp.exp(s - m)
    l = jnp.sum(p, axis=-1, keepdims=True)
    return p.astype(BF16), m, l


def _attn_kernel(q0, k0, v0, q1, k1, v1, q2, k2, v2, b01_ref, b2_ref, o_ref, osc, lsc, *, seq):
    u2, u1, u0 = ATTN_UNROLL
    cls_len = seq // CLS
    sub = QBLK // 4
    ksub = KWIN // 4

    def g2_body(it, carry):
        cs = [it * u2 + u for u in range(u2)]
        ss = [_scores(q2[c], k2[c], b2_ref[...]) for c in cs]
        pml = [_softmax(s) for s in ss]
        outs = [jnp.dot(p, v2[c], preferred_element_type=F32) for (p, _, _), c in zip(pml, cs)]
        for c, o, (_, m, l) in zip(cs, outs, pml):
            rows = pl.ds(c, cls_len, stride=CLS)
            osc[1, rows, :] = o * (1.0 / l)
            lsc[1, rows, :] = jnp.broadcast_to(m + jnp.log(l), (QBLK, LANES))
        return carry

    lax.fori_loop(0, CLS // u2, g2_body, 0)

    nblk1 = (seq // 4) // QBLK
    kss1 = [min(max(sub * i - BAND // 4, 0), cls_len - ksub) for i in range(nblk1)]
    var1 = [0 if i == 0 else (2 if i == nblk1 - 1 else 1) for i in range(nblk1)]

    def g1_body(it, carry):
        blocks = [(it * u1 + uu, i) for uu in range(u1) for i in range(nblk1)]
        ss = []
        for c4, i in blocks:
            q = jnp.concatenate([q1[4 * a + c4, sub * i:sub * (i + 1), :] for a in range(4)], axis=0)
            k = jnp.concatenate([k1[4 * a + c4, kss1[i]:kss1[i] + ksub, :] for a in range(4)], axis=0)
            ss.append(_scores(q, k, b01_ref[1, var1[i]]))
        pml = [_softmax(s) for s in ss]
        outs = []
        for (c4, i), (p, _, _) in zip(blocks, pml):
            v = jnp.concatenate([v1[4 * a + c4, kss1[i]:kss1[i] + ksub, :] for a in range(4)], axis=0)
            outs.append(jnp.dot(p, v, preferred_element_type=F32))
        for (c4, i), o, (_, m, l) in zip(blocks, outs, pml):
            o = o * (1.0 / l)
            lse = jnp.broadcast_to(m + jnp.log(l), (QBLK, LANES))
            for a in range(4):
                rows = pl.ds(CLS * sub * i + 4 * a + c4, sub, stride=CLS)
                osc[0, rows, :] = o[sub * a:sub * (a + 1)]
                lsc[0, rows, :] = lse[sub * a:sub * (a + 1)]
        return carry

    lax.fori_loop(0, 4 // u1, g1_body, 0)

    nblk0 = seq // QBLK

    def g0_body(it, carry):
        idx = [it * u0 + u for u in range(u0)]
        qss = [pl.multiple_of(i * QBLK, QBLK) for i in idx]
        kss = [pl.multiple_of(jnp.clip(i * QBLK - BAND, 0, seq - KWIN), BAND) for i in idx]
        var = [jnp.where(i == 0, 0, jnp.where(i == nblk0 - 1, 2, 1)) for i in idx]
        ss = [_scores(q0[pl.ds(qs, QBLK), :], k0[pl.ds(ks, KWIN), :], b01_ref[0, vr])
              for qs, ks, vr in zip(qss, kss, var)]
        pml = [_softmax(s) for s in ss]
        outs = [jnp.dot(p, v0[pl.ds(ks, KWIN), :], preferred_element_type=F32)
                for (p, _, _), ks in zip(pml, kss)]
        for qs, o0, (_, m0, l0) in zip(qss, outs, pml):
            rows = pl.ds(qs, QBLK)
            lse1 = lsc[0, rows, :]
            lse2 = lsc[1, rows, :]
            mx = jnp.maximum(jnp.maximum(lse1, lse2), m0)
            w0 = jnp.exp(m0 - mx)
            w1 = jnp.exp(lse1 - mx)
            w2 = jnp.exp(lse2 - mx)
            num = w0 * o0 + w1 * osc[0, rows, :] + w2 * osc[1, rows, :]
            den = w0 * l0 + w1 + w2
            o_ref[rows, :] = (num / den).astype(BF16)
        return carry

    lax.fori_loop(0, nblk0 // u0, g0_body, 0)


def _attention(z0, z12, b01, b2, batch, seq):
    in_specs = []
    args = []
    for t in range(3):
        in_specs.append(pl.BlockSpec((None, None, seq, LANES),
                                     lambda h, b, t=t: (b, t * N_HEADS + h, 0, 0)))
        args.append(z0)
    for g in range(2):
        for t in range(3):
            in_specs.append(pl.BlockSpec((None, None, CLS, seq // CLS, LANES),
                                         lambda h, b, t=t, g=g: (b, (3 * g + t) * N_HEADS + h, 0, 0, 0)))
            args.append(z12)
    in_specs.append(pl.BlockSpec((None, 2, 3, QBLK, KWIN), lambda h, b: (h, 0, 0, 0, 0)))
    in_specs.append(pl.BlockSpec((None, QBLK, QBLK), lambda h, b: (h, 0, 0)))
    args += [b01, b2]
    vmem = (2 * 9 * seq * LANES * 2 + 2 * 7 * QBLK * KWIN * 4 + 2 * seq * LANES * 2
            + 4 * seq * LANES * 4)
    return pl.pallas_call(
        functools.partial(_attn_kernel, seq=seq),
        grid=(N_HEADS, batch),
        in_specs=in_specs,
        out_specs=pl.BlockSpec((None, None, seq, LANES), lambda h, b: (b, h, 0, 0)),
        out_shape=jax.ShapeDtypeStruct((batch, N_HEADS, seq, LANES), BF16),
        scratch_shapes=[pltpu.VMEM((2, seq, LANES), F32)] * 2,
        compiler_params=pltpu.CompilerParams(
            dimension_semantics=("parallel", "arbitrary"),
            vmem_limit_bytes=_vmem_limit(vmem + (16 << 20))),
        name="attn",
    )(*args)


def _mix_out_kernel(x_ref, att_ref, up_ref, u_ref, un_ref, wp_ref, ps_ref, wo_ref, o_ref, ext_ref,
                    *, seq):
    i = pl.program_id(0)
    tm = x_ref.shape[0]
    pos0 = (i % (seq // tm)) * tm
    ext_ref[0:POOL_HALO, :] = jnp.where(pos0 > 0, up_ref[...], 0.0)
    ext_ref[POOL_HALO:POOL_HALO + tm, :] = u_ref[...]
    ext_ref[POOL_HALO + tm:2 * POOL_HALO + tm, :] = jnp.where(pos0 + tm < seq, un_ref[...], 0.0)
    iota = lax.broadcasted_iota(jnp.int32, (POOL_ROWS, 1), 0)

    att = jnp.concatenate([att_ref[h] for h in range(N_HEADS)], axis=-1)
    acc = x_ref[...] + jnp.dot(att, wo_ref[0:ATT_WIDTH, :], preferred_element_type=F32)
    pooled = []
    for g, w in enumerate(POOL_WINDOWS):
        h = w // 2
        cols = slice(g * POOL_GROUP, (g + 1) * POOL_GROUP)
        dgs = []
        for r0 in range(0, tm, POOL_ROWS):
            base = POOL_HALO + r0
            ssum = ext_ref[base - h:base - h + POOL_ROWS, cols]
            for kk in range(-h + 1, h + 1):
                ssum = ssum + ext_ref[base + kk:base + kk + POOL_ROWS, cols]
            pos = pos0 + r0 + iota
            cnt = (jnp.minimum(pos + h + 1, seq) - jnp.maximum(pos - h, 0)).astype(F32)
            dgs.append((ssum / cnt - u_ref[r0:r0 + POOL_ROWS, cols]).astype(BF16))
        y = jnp.dot(jnp.concatenate(dgs, axis=0), wp_ref[g], preferred_element_type=F32)
        pooled.append((y * ps_ref[:, cols]).astype(BF16))
    pooled = jnp.concatenate(pooled, axis=-1)
    o_ref[...] = acc + jnp.dot(pooled, wo_ref[ATT_WIDTH:, :], preferred_element_type=F32)


def _mix_out(x, att, u, w_pool, pool_scale, w_out, layer, seq, *, tm=ROW_CHUNK):
    t, d = x.shape
    tps = seq // tm
    hb = tm // POOL_HALO
    nhb = t // POOL_HALO
    vmem = 4 * tm * d * 4 + 2 * tm * ATT_WIDTH * 2 + 3 * tm * POOL_WIDTH * 4 + 2 * d * d * 2 + 2 * tm * d * 4
    return pl.pallas_call(
        functools.partial(_mix_out_kernel, seq=seq),
        grid=(t // tm,),
        in_specs=[
            pl.BlockSpec((tm, d), lambda i: (i, 0)),
            pl.BlockSpec((None, N_HEADS, tm, LANES), lambda i: (i // tps, 0, i % tps, 0)),
            pl.BlockSpec((POOL_HALO, POOL_WIDTH), lambda i: (jnp.maximum(i * hb - 1, 0), 0)),
            pl.BlockSpec((tm, POOL_WIDTH), lambda i: (i, 0)),
            pl.BlockSpec((POOL_HALO, POOL_WIDTH), lambda i: (jnp.minimum((i + 1) * hb, nhb - 1), 0)),
            pl.BlockSpec((None, len(POOL_WINDOWS), POOL_GROUP, POOL_GROUP), lambda i: (layer, 0, 0, 0)),
            pl.BlockSpec((None, 1, POOL_WIDTH), lambda i: (layer, 0, 0)),
            pl.BlockSpec((None, d, d), lambda i: (layer, 0, 0)),
        ],
        out_specs=pl.BlockSpec((tm, d), lambda i: (i, 0)),
        out_shape=jax.ShapeDtypeStruct((t, d), F32),
        scratch_shapes=[pltpu.VMEM((tm + 2 * POOL_HALO, POOL_WIDTH), F32)],
        compiler_params=pltpu.CompilerParams(
            dimension_semantics=("parallel",),
            vmem_limit_bytes=_vmem_limit(vmem + (8 << 20))),
        name="mix_out",
    )(x, att, u, u, u, w_pool, pool_scale, w_out)


def _t5_bucket(rel):
    half = N_BUCKETS // 2
    max_exact = half // 2
    ret = jnp.where(rel > 0, half, 0)
    n = jnp.abs(rel)
    nf = jnp.maximum(n, 1).astype(F32)
    large = max_exact + (jnp.log(nf / max_exact) / math.log(MAX_DISTANCE / max_exact)
                         * (half - max_exact)).astype(jnp.int32)
    large = jnp.minimum(large, half - 1)
    return ret + jnp.where(n < max_exact, n, large)


def _bias_lookup(table, delta, dil):
    onehot = (_t5_bucket(delta * dil)[..., None] == jnp.arange(N_BUCKETS)).astype(F32)
    b = jnp.einsum("...k,kh->h...", onehot, table, precision=lax.Precision.HIGHEST)
    return jnp.where(jnp.abs(delta) <= BAND, b, NEG_INF)


def _bias_tiles(rel_bias):
    rb = rel_bias.astype(F32)
    qi = jnp.arange(QBLK)[:, None]
    kj = jnp.arange(KWIN)[None, :]
    q1 = 4 * (qi % (QBLK // 4)) + qi // (QBLK // 4)
    k1 = 4 * (kj % (KWIN // 4)) + kj // (KWIN // 4)
    g0 = jnp.stack([_bias_lookup(rb[:, 0:N_HEADS], kj - qi - v * BAND, 1) for v in range(3)], axis=1)
    g1 = jnp.stack([_bias_lookup(rb[:, N_HEADS:2 * N_HEADS], k1 - q1 - v * BAND, 4)
                    for v in range(3)], axis=1)
    b2 = _bias_lookup(rb[:, 2 * N_HEADS:3 * N_HEADS], jnp.arange(QBLK)[None, :] - qi, 16)
    return jnp.stack([g0, g1], axis=1), b2


def _trunk(x3, p):
    batch, seq, d = x3.shape
    x = x3.reshape(batch * seq, d)
    depth = p["w_in"].shape[0]
    for l in range(depth):
        x = _ffn(x, p["norm_g"], p["ffn_gate"], p["ffn_up"], p["ffn_down"], l, 0, 0)
        z0, u = _proj_nat(x, p["norm_g"], p["w_in"], l, batch, seq)
        z12 = _proj_cls(x, p["norm_g"], p["w_in"], p["perm"], l, batch, seq)
        att = _attention(z0, z12, p["b01"], p["b2"], batch, seq)
        x = _mix_out(x, att, u, p["w_pool"], p["pool_scale"], p["w_out"], l, seq)
        x = _ffn(x, p["norm_g"], p["ffn_gate"], p["ffn_up"], p["ffn_down"], l, 1, 2,
                 p["final_g"] if l == depth - 1 else None)
    return x.reshape(batch, seq, d)


def kernel(x_prompt, x_sample, norm_g, ffn_gate, ffn_up, ffn_down, w_in, w_pool, pool_scale,
           w_out, rel_bias, final_g):
    depth = w_in.shape[0]
    b01, b2 = _bias_tiles(rel_bias)
    p = {
        "norm_g": norm_g.reshape(depth, 3, 1, D_MODEL),
        "ffn_gate": ffn_gate.astype(BF16),
        "ffn_up": ffn_up.astype(BF16),
        "ffn_down": ffn_down.astype(BF16),
        "w_in": w_in.astype(BF16),
        "w_pool": w_pool.astype(BF16),
        "pool_scale": pool_scale.reshape(depth, 1, POOL_WIDTH),
        "w_out": w_out.astype(BF16),
        "b01": b01,
        "b2": b2,
        "perm": _class_major_perm(ROW_CHUNK),
        "final_g": final_g.reshape(1, D_MODEL),
    }
    return (_trunk(x_prompt, p), _trunk(x_sample, p))
```

```python
import functools
import math

import jax
import jax.numpy as jnp
from jax import lax
from jax.experimental import pallas as pl
from jax.experimental.pallas import tpu as pltpu

F32 = jnp.float32
BF16 = jnp.bfloat16

D_MODEL = 2048
HEAD_DIM = 128
N_HEADS = 12
ATT_WIDTH = N_HEADS * HEAD_DIM
POOL_WIDTH = D_MODEL - ATT_WIDTH
POOL_WINDOWS = (2, 4, 8, 16)
POOL_GROUP = POOL_WIDTH // len(POOL_WINDOWS)
POOL_HALO = 8
POOL_ROWS = 128
N_GROUPS = 3
CLS = 16
BAND = 64
GROUP_WIDTH = 3 * ATT_WIDTH
QKV_WIDTH = N_GROUPS * GROUP_WIDTH
N_BUCKETS = 32
MAX_DISTANCE = 1024
EPS = 1e-6
NEG_INF = -1e30
LANES = 128
QBLK = 128
KWIN = QBLK + 2 * BAND
TOKEN_TILE = 1024
ROW_CHUNK = 512
V7X_VMEM_BYTES = 64 * 1024 * 1024


def _vmem_limit(nbytes):
    return int(min(nbytes, V7X_VMEM_BYTES - 6 * 1024 * 1024))


def _rms_scale(x):
    return lax.rsqrt(jnp.mean(x * x, axis=-1, keepdims=True) + EPS)


def _ffn_kernel(*refs, final):
    if final:
        x_ref, g_ref, wg_ref, wu_ref, wd_ref, fg_ref, o_ref, xn_ref = refs
    else:
        x_ref, g_ref, wg_ref, wu_ref, wd_ref, o_ref, xn_ref = refs
    j = pl.program_id(1)

    @pl.when(j == 0)
    def _():
        x = x_ref[...]
        xn_ref[...] = (x * _rms_scale(x) * g_ref[...]).astype(BF16)
        o_ref[...] = x

    for r in range(x_ref.shape[0] // ROW_CHUNK):
        rows = slice(r * ROW_CHUNK, (r + 1) * ROW_CHUNK)
        xn = xn_ref[rows, :]
        a = jnp.dot(xn, wg_ref[...], preferred_element_type=F32)
        b = jnp.dot(xn, wu_ref[...], preferred_element_type=F32)
        h = (a * jax.nn.sigmoid(a)) * (0.5 * b)
        o_ref[rows, :] += jnp.dot(h.astype(BF16), wd_ref[...], preferred_element_type=F32)

    if final:
        @pl.when(j == pl.num_programs(1) - 1)
        def _():
            y = o_ref[...]
            o_ref[...] = y * _rms_scale(y) * fg_ref[...]


def _ffn(x, norm_g, wg, wu, wd, layer, which, norm_idx, final_g=None, *, tm=TOKEN_TILE, tf=512):
    t, d = x.shape
    f = wg.shape[-1]
    final = final_g is not None
    in_specs = [
        pl.BlockSpec((tm, d), lambda i, j: (i, 0)),
        pl.BlockSpec((None, None, 1, d), lambda i, j: (layer, norm_idx, 0, 0)),
        pl.BlockSpec((None, None, d, tf), lambda i, j: (layer, which, 0, j)),
        pl.BlockSpec((None, None, d, tf), lambda i, j: (layer, which, 0, j)),
        pl.BlockSpec((None, None, tf, d), lambda i, j: (layer, which, j, 0)),
    ]
    args = [x, norm_g, wg, wu, wd]
    if final:
        in_specs.append(pl.BlockSpec((1, d), lambda i, j: (0, 0)))
        args.append(final_g)
    vmem = 4 * tm * d * 4 + tm * d * 2 + 2 * 3 * d * tf * 2 + 4 * ROW_CHUNK * tf * 4
    return pl.pallas_call(
        functools.partial(_ffn_kernel, final=final),
        grid=(t // tm, f // tf),
        in_specs=in_specs,
        out_specs=pl.BlockSpec((tm, d), lambda i, j: (i, 0)),
        out_shape=jax.ShapeDtypeStruct((t, d), F32),
        scratch_shapes=[pltpu.VMEM((tm, d), BF16)],
        compiler_params=pltpu.CompilerParams(
            dimension_semantics=("parallel", "arbitrary"),
            vmem_limit_bytes=_vmem_limit(vmem + (8 << 20))),
        name="ffn_final" if final else "ffn",
    )(*args)


def _proj_nat_kernel(x_ref, g_ref, w_ref, wu_ref, z_ref, u_ref, xn_ref):
    j = pl.program_id(1)
    chunks = [slice(r * ROW_CHUNK, (r + 1) * ROW_CHUNK) for r in range(x_ref.shape[0] // ROW_CHUNK)]

    @pl.when(j == 0)
    def _():
        for rows in chunks:
            x = x_ref[rows, :]
            xn_ref[rows, :] = (x * _rms_scale(x) * g_ref[...]).astype(BF16)
            u_ref[rows, :] = jnp.dot(xn_ref[rows, :], wu_ref[...], preferred_element_type=F32)

    for rows in chunks:
        res = jnp.dot(xn_ref[rows, :], w_ref[...], preferred_element_type=F32)
        for kk in range(z_ref.shape[0]):
            z_ref[kk, rows, :] = res[:, kk * LANES:(kk + 1) * LANES].astype(BF16)


def _proj_cls_kernel(x_ref, g_ref, perm_ref, w_ref, z_ref, xn_ref):
    j = pl.program_id(1)
    n = ROW_CHUNK // CLS
    chunks = [slice(r * ROW_CHUNK, (r + 1) * ROW_CHUNK) for r in range(x_ref.shape[0] // ROW_CHUNK)]

    @pl.when(j == 0)
    def _():
        for rows in chunks:
            x = x_ref[rows, :]
            xn = (x * _rms_scale(x) * g_ref[...]).astype(BF16)
            xn_ref[rows, :] = jnp.dot(perm_ref[...], xn, preferred_element_type=F32).astype(BF16)

    for r, rows in enumerate(chunks):
        res = jnp.dot(xn_ref[rows, :], w_ref[...], preferred_element_type=F32)
        for kk in range(z_ref.shape[0]):
            for c in range(CLS):
                z_ref[kk, c, r * n:(r + 1) * n, :] = (
                    res[c * n:(c + 1) * n, kk * LANES:(kk + 1) * LANES].astype(BF16))


def _proj_vmem(tm, d, tn):
    return 2 * tm * d * 4 + tm * d * 2 + 2 * d * tn * 2 + 2 * tm * tn * 2 + 2 * ROW_CHUNK * tn * 4


def _proj_nat(x, norm_g, w_in, layer, batch, seq, *, tm=TOKEN_TILE):
    t, d = x.shape
    tn = ATT_WIDTH
    tps = seq // tm
    ucb = QKV_WIDTH // POOL_WIDTH
    return pl.pallas_call(
        _proj_nat_kernel,
        grid=(t // tm, 2),
        in_specs=[
            pl.BlockSpec((tm, d), lambda i, j: (i, 0)),
            pl.BlockSpec((None, None, 1, d), lambda i, j: (layer, 1, 0, 0)),
            pl.BlockSpec((None, d, tn), lambda i, j: (layer, 0, 1 + j)),
            pl.BlockSpec((None, d, POOL_WIDTH), lambda i, j: (layer, 0, ucb)),
        ],
        out_specs=[
            pl.BlockSpec((None, N_HEADS, tm, LANES), lambda i, j: (i // tps, j, i % tps, 0)),
            pl.BlockSpec((tm, POOL_WIDTH), lambda i, j: (i, 0)),
        ],
        out_shape=[
            jax.ShapeDtypeStruct((batch, 2 * N_HEADS, seq, LANES), BF16),
            jax.ShapeDtypeStruct((t, POOL_WIDTH), F32),
        ],
        scratch_shapes=[pltpu.VMEM((tm, d), BF16)],
        compiler_params=pltpu.CompilerParams(
            dimension_semantics=("parallel", "arbitrary"),
            vmem_limit_bytes=_vmem_limit(_proj_vmem(tm, d, tn) + 4 * tm * POOL_WIDTH * 4 + (8 << 20))),
        name="proj_nat",
    )(x, norm_g, w_in, w_in)


def _class_major_perm(tm):
    r = jnp.arange(tm)
    src = (r % (tm // CLS)) * CLS + r // (tm // CLS)
    return (src[:, None] == r[None, :]).astype(BF16)


N_CLS_SLABS = 7


def _proj_cls(x, norm_g, w_in, perm, layer, batch, seq, *, tm=TOKEN_TILE):
    t, d = x.shape
    tn = ATT_WIDTH
    tps = seq // tm
    return pl.pallas_call(
        _proj_cls_kernel,
        grid=(t // tm, N_CLS_SLABS),
        in_specs=[
            pl.BlockSpec((tm, d), lambda i, j: (i, 0)),
            pl.BlockSpec((None, None, 1, d), lambda i, j: (layer, 1, 0, 0)),
            pl.BlockSpec((ROW_CHUNK, ROW_CHUNK), lambda i, j: (0, 0)),
            pl.BlockSpec((None, d, tn), lambda i, j: (layer, 0, jnp.where(j == 0, 0, j + 2))),
        ],
        out_specs=pl.BlockSpec((None, N_HEADS, CLS, tm // CLS, LANES),
                               lambda i, j: (i // tps, j, 0, i % tps, 0)),
        out_shape=jax.ShapeDtypeStruct((batch, N_CLS_SLABS * N_HEADS, CLS, seq // CLS, LANES), BF16),
        scratch_shapes=[pltpu.VMEM((tm, d), BF16)],
        compiler_params=pltpu.CompilerParams(
            dimension_semantics=("parallel", "arbitrary"),
            vmem_limit_bytes=_vmem_limit(_proj_vmem(tm, d, tn) + 2 * ROW_CHUNK * ROW_CHUNK * 2 + (8 << 20))),
        name="proj_cls",
    )(x, norm_g, perm, w_in)


ATTN_UNROLL = (16, 4, 8)


def _scores(q, k, bias):
    s = lax.dot_general(q, k, (((1,), (1,)), ((), ())), preferred_element_type=F32)
    return s * (HEAD_DIM ** -0.5) + bias


def _softmax(s):
    m = jnp.max(s, axis=-1, keepdims=True)
    p = jnp.exp(s - m)
    l = jnp.sum(p, axis=-1, keepdims=True)
    return p.astype(BF16), m, l


def _attn_kernel(q0c, k0, v0, q1, k1, v1, q2, k2, v2, b01_ref, b2_ref, unperm_ref, o_ref,
                 osc, lsc, q0f, *, seq):
    u2, u1, u0 = ATTN_UNROLL
    cls_len = seq // CLS
    sub = QBLK // 4
    ksub = KWIN // 4
    rpc = QBLK // CLS

    def g2_body(it, carry):
        cs = [it * u2 + u for u in range(u2)]
        ss = [_scores(q2[c], k2[c], b2_ref[...]) for c in cs]
        pml = [_softmax(s) for s in ss]
        outs = [jnp.dot(p, v2[c], preferred_element_type=F32) for (p, _, _), c in zip(pml, cs)]
        for c, o, (_, m, l) in zip(cs, outs, pml):
            osc[1, c] = o * (1.0 / l)
            lsc[1, c] = jnp.broadcast_to(m + jnp.log(l), (QBLK, LANES))
            q0f[c] = q0c[c].astype(F32)
        return carry

    lax.fori_loop(0, CLS // u2, g2_body, 0)

    nblk1 = (seq // 4) // QBLK
    kss1 = [min(max(sub * i - BAND // 4, 0), cls_len - ksub) for i in range(nblk1)]
    var1 = [0 if i == 0 else (2 if i == nblk1 - 1 else 1) for i in range(nblk1)]

    def g1_body(it, carry):
        blocks = [(it * u1 + uu, i) for uu in range(u1) for i in range(nblk1)]
        ss = []
        for c4, i in blocks:
            q = jnp.concatenate([q1[4 * a + c4, sub * i:sub * (i + 1), :] for a in range(4)], axis=0)
            k = jnp.concatenate([k1[4 * a + c4, kss1[i]:kss1[i] + ksub, :] for a in range(4)], axis=0)
            ss.append(_scores(q, k, b01_ref[1, var1[i]]))
        pml = [_softmax(s) for s in ss]
        outs = []
        for (c4, i), (p, _, _) in zip(blocks, pml):
            v = jnp.concatenate([v1[4 * a + c4, kss1[i]:kss1[i] + ksub, :] for a in range(4)], axis=0)
            outs.append(jnp.dot(p, v, preferred_element_type=F32))
        for (c4, i), o, (_, m, l) in zip(blocks, outs, pml):
            o = o * (1.0 / l)
            lse = jnp.broadcast_to(m + jnp.log(l), (QBLK, LANES))
            for a in range(4):
                osc[0, 4 * a + c4, sub * i:sub * (i + 1), :] = o[sub * a:sub * (a + 1)]
                lsc[0, 4 * a + c4, sub * i:sub * (i + 1), :] = lse[sub * a:sub * (a + 1)]
        return carry

    lax.fori_loop(0, 4 // u1, g1_body, 0)

    nblk0 = seq // QBLK

    def class_rows(ref, lead, r0):
        return jnp.concatenate([ref[lead + (c, pl.ds(r0, rpc), slice(None))] for c in range(CLS)], axis=0)

    def g0_body(it, carry):
        idx = [it * u0 + u for u in range(u0)]
        r0s = [pl.multiple_of(i * rpc, rpc) for i in idx]
        kss = [pl.multiple_of(jnp.clip(i * QBLK - BAND, 0, seq - KWIN), BAND) for i in idx]
        var = [jnp.where(i == 0, 0, jnp.where(i == nblk0 - 1, 2, 1)) for i in idx]
        ss = [_scores(class_rows(q0f, (), r0).astype(BF16), k0[pl.ds(ks, KWIN), :], b01_ref[0, vr])
              for r0, ks, vr in zip(r0s, kss, var)]
        pml = [_softmax(s) for s in ss]
        outs = [jnp.dot(p, v0[pl.ds(ks, KWIN), :], preferred_element_type=F32)
                for (p, _, _), ks in zip(pml, kss)]
        atts = []
        for r0, o0, (_, m0, l0) in zip(r0s, outs, pml):
            lse1 = class_rows(lsc, (0,), r0)
            lse2 = class_rows(lsc, (1,), r0)
            mx = jnp.maximum(jnp.maximum(lse1, lse2), m0)
            w0 = jnp.exp(m0 - mx)
            w1 = jnp.exp(lse1 - mx)
            w2 = jnp.exp(lse2 - mx)
            num = w0 * o0 + w1 * class_rows(osc, (0,), r0) + w2 * class_rows(osc, (1,), r0)
            den = w0 * l0 + w1 + w2
            atts.append((num / den).astype(BF16))
        for i, att in zip(idx, atts):
            nat = jnp.dot(unperm_ref[...], att, preferred_element_type=F32)
            o_ref[pl.ds(pl.multiple_of(i * QBLK, QBLK), QBLK), :] = nat.astype(BF16)
        return carry

    lax.fori_loop(0, nblk0 // u0, g0_body, 0)


def _attention(z0kv, zc, b01, b2, unperm, batch, seq):
    def cls_spec(slab):
        return pl.BlockSpec((None, None, CLS, seq // CLS, LANES),
                            lambda h, b: (b, slab * N_HEADS + h, 0, 0, 0))

    def nat_spec(t):
        return pl.BlockSpec((None, None, seq, LANES), lambda h, b: (b, t * N_HEADS + h, 0, 0))

    in_specs = [cls_spec(0), nat_spec(0), nat_spec(1)] + [cls_spec(slab) for slab in range(1, N_CLS_SLABS)]
    in_specs += [
        pl.BlockSpec((None, 2, 3, QBLK, KWIN), lambda h, b: (h, 0, 0, 0, 0)),
        pl.BlockSpec((None, QBLK, QBLK), lambda h, b: (h, 0, 0)),
        pl.BlockSpec((QBLK, QBLK), lambda h, b: (0, 0)),
    ]
    args = [zc, z0kv, z0kv] + [zc] * (N_CLS_SLABS - 1) + [b01, b2, unperm]
    vmem = (2 * 9 * seq * LANES * 2 + 2 * 8 * QBLK * KWIN * 4 + 2 * seq * LANES * 2
            + 5 * seq * LANES * 4)
    return pl.pallas_call(
        functools.partial(_attn_kernel, seq=seq),
        grid=(N_HEADS, batch),
        in_specs=in_specs,
        out_specs=pl.BlockSpec((None, None, seq, LANES), lambda h, b: (b, h, 0, 0)),
        out_shape=jax.ShapeDtypeStruct((batch, N_HEADS, seq, LANES), BF16),
        scratch_shapes=[pltpu.VMEM((2, CLS, seq // CLS, LANES), F32),
                        pltpu.VMEM((2, CLS, seq // CLS, LANES), F32),
                        pltpu.VMEM((CLS, seq // CLS, LANES), F32)],
        compiler_params=pltpu.CompilerParams(
            dimension_semantics=("parallel", "arbitrary"),
            vmem_limit_bytes=_vmem_limit(vmem + (16 << 20))),
        name="attn",
    )(*args)


def _mix_out_kernel(x_ref, att_ref, up_ref, u_ref, un_ref, wp_ref, ps_ref, wo_ref, o_ref, ext_ref,
                    *, seq):
    i = pl.program_id(0)
    tm = x_ref.shape[0]
    pos0 = (i % (seq // tm)) * tm
    ext_ref[0:POOL_HALO, :] = jnp.where(pos0 > 0, up_ref[...], 0.0)
    ext_ref[POOL_HALO:POOL_HALO + tm, :] = u_ref[...]
    ext_ref[POOL_HALO + tm:2 * POOL_HALO + tm, :] = jnp.where(pos0 + tm < seq, un_ref[...], 0.0)
    iota = lax.broadcasted_iota(jnp.int32, (POOL_ROWS, 1), 0)

    att = jnp.concatenate([att_ref[h] for h in range(N_HEADS)], axis=-1)
    acc = x_ref[...] + jnp.dot(att, wo_ref[0:ATT_WIDTH, :], preferred_element_type=F32)
    pooled = []
    for g, w in enumerate(POOL_WINDOWS):
        h = w // 2
        cols = slice(g * POOL_GROUP, (g + 1) * POOL_GROUP)
        dgs = []
        for r0 in range(0, tm, POOL_ROWS):
            base = POOL_HALO + r0
            ssum = ext_ref[base - h:base - h + POOL_ROWS, cols]
            for kk in range(-h + 1, h + 1):
                ssum = ssum + ext_ref[base + kk:base + kk + POOL_ROWS, cols]
            pos = pos0 + r0 + iota
            cnt = (jnp.minimum(pos + h + 1, seq) - jnp.maximum(pos - h, 0)).astype(F32)
            dgs.append((ssum / cnt - u_ref[r0:r0 + POOL_ROWS, cols]).astype(BF16))
        y = jnp.dot(jnp.concatenate(dgs, axis=0), wp_ref[g], preferred_element_type=F32)
        pooled.append((y * ps_ref[:, cols]).astype(BF16))
    pooled = jnp.concatenate(pooled, axis=-1)
    o_ref[...] = acc + jnp.dot(pooled, wo_ref[ATT_WIDTH:, :], preferred_element_type=F32)


def _mix_out(x, att, u, w_pool, pool_scale, w_out, layer, seq, *, tm=ROW_CHUNK):
    t, d = x.shape
    tps = seq // tm
    hb = tm // POOL_HALO
    nhb = t // POOL_HALO
    vmem = 4 * tm * d * 4 + 2 * tm * ATT_WIDTH * 2 + 3 * tm * POOL_WIDTH * 4 + 2 * d * d * 2 + 2 * tm * d * 4
    return pl.pallas_call(
        functools.partial(_mix_out_kernel, seq=seq),
        grid=(t // tm,),
        in_specs=[
            pl.BlockSpec((tm, d), lambda i: (i, 0)),
            pl.BlockSpec((None, N_HEADS, tm, LANES), lambda i: (i // tps, 0, i % tps, 0)),
            pl.BlockSpec((POOL_HALO, POOL_WIDTH), lambda i: (jnp.maximum(i * hb - 1, 0), 0)),
            pl.BlockSpec((tm, POOL_WIDTH), lambda i: (i, 0)),
            pl.BlockSpec((POOL_HALO, POOL_WIDTH), lambda i: (jnp.minimum((i + 1) * hb, nhb - 1), 0)),
            pl.BlockSpec((None, len(POOL_WINDOWS), POOL_GROUP, POOL_GROUP), lambda i: (layer, 0, 0, 0)),
            pl.BlockSpec((None, 1, POOL_WIDTH), lambda i: (layer, 0, 0)),
            pl.BlockSpec((None, d, d), lambda i: (layer, 0, 0)),
        ],
        out_specs=pl.BlockSpec((tm, d), lambda i: (i, 0)),
        out_shape=jax.ShapeDtypeStruct((t, d), F32),
        scratch_shapes=[pltpu.VMEM((tm + 2 * POOL_HALO, POOL_WIDTH), F32)],
        compiler_params=pltpu.CompilerParams(
            dimension_semantics=("parallel",),
            vmem_limit_bytes=_vmem_limit(vmem + (8 << 20))),
        name="mix_out",
    )(x, att, u, u, u, w_pool, pool_scale, w_out)


def _t5_bucket(rel):
    half = N_BUCKETS // 2
    max_exact = half // 2
    ret = jnp.where(rel > 0, half, 0)
    n = jnp.abs(rel)
    nf = jnp.maximum(n, 1).astype(F32)
    large = max_exact + (jnp.log(nf / max_exact) / math.log(MAX_DISTANCE / max_exact)
                         * (half - max_exact)).astype(jnp.int32)
    large = jnp.minimum(large, half - 1)
    return ret + jnp.where(n < max_exact, n, large)


def _bias_lookup(table, delta, dil):
    onehot = (_t5_bucket(delta * dil)[..., None] == jnp.arange(N_BUCKETS)).astype(F32)
    b = jnp.einsum("...k,kh->h...", onehot, table, precision=lax.Precision.HIGHEST)
    return jnp.where(jnp.abs(delta) <= BAND, b, NEG_INF)


def _block_class_positions():
    r = jnp.arange(QBLK)
    rpc = QBLK // CLS
    return CLS * (r % rpc) + r // rpc


def _bias_tiles(rel_bias):
    rb = rel_bias.astype(F32)
    qi = jnp.arange(QBLK)[:, None]
    kj = jnp.arange(KWIN)[None, :]
    q0 = _block_class_positions()[:, None]
    q1 = 4 * (qi % (QBLK // 4)) + qi // (QBLK // 4)
    k1 = 4 * (kj % (KWIN // 4)) + kj // (KWIN // 4)
    g0 = jnp.stack([_bias_lookup(rb[:, 0:N_HEADS], kj - q0 - v * BAND, 1) for v in range(3)], axis=1)
    g1 = jnp.stack([_bias_lookup(rb[:, N_HEADS:2 * N_HEADS], k1 - q1 - v * BAND, 4)
                    for v in range(3)], axis=1)
    b2 = _bias_lookup(rb[:, 2 * N_HEADS:3 * N_HEADS], jnp.arange(QBLK)[None, :] - qi, 16)
    return jnp.stack([g0, g1], axis=1), b2


def _block_unperm():
    return (_block_class_positions()[None, :] == jnp.arange(QBLK)[:, None]).astype(BF16)


def _trunk(x3, p):
    batch, seq, d = x3.shape
    x = x3.reshape(batch * seq, d)
    depth = p["w_in"].shape[0]
    for l in range(depth):
        x = _ffn(x, p["norm_g"], p["ffn_gate"], p["ffn_up"], p["ffn_down"], l, 0, 0)
        z0kv, u = _proj_nat(x, p["norm_g"], p["w_in"], l, batch, seq)
        zc = _proj_cls(x, p["norm_g"], p["w_in"], p["perm"], l, batch, seq)
        att = _attention(z0kv, zc, p["b01"], p["b2"], p["unperm"], batch, seq)
        x = _mix_out(x, att, u, p["w_pool"], p["pool_scale"], p["w_out"], l, seq)
        x = _ffn(x, p["norm_g"], p["ffn_gate"], p["ffn_up"], p["ffn_down"], l, 1, 2,
                 p["final_g"] if l == depth - 1 else None)
    return x.reshape(batch, seq, d)


def kernel(x_prompt, x_sample, norm_g, ffn_gate, ffn_up, ffn_down, w_in, w_pool, pool_scale,
           w_out, rel_bias, final_g):
    depth = w_in.shape[0]
    b01, b2 = _bias_tiles(rel_bias)
    p = {
        "norm_g": norm_g.reshape(depth, 3, 1, D_MODEL),
        "ffn_gate": ffn_gate.astype(BF16),
        "ffn_up": ffn_up.astype(BF16),
        "ffn_down": ffn_down.astype(BF16),
        "w_in": w_in.astype(BF16),
        "w_pool": w_pool.astype(BF16),
        "pool_scale": pool_scale.reshape(depth, 1, POOL_WIDTH),
        "w_out": w_out.astype(BF16),
        "b01": b01,
        "b2": b2,
        "perm": _class_major_perm(ROW_CHUNK),
        "unperm": _block_unperm(),
        "final_g": final_g.reshape(1, D_MODEL),
    }
    return (_trunk(x_prompt, p), _trunk(x_sample, p))
```

```python
import functools
import math

import jax
import jax.numpy as jnp
from jax import lax
from jax.experimental import pallas as pl
from jax.experimental.pallas import tpu as pltpu

F32 = jnp.float32
BF16 = jnp.bfloat16

D_MODEL = 2048
HEAD_DIM = 128
N_HEADS = 12
ATT_WIDTH = N_HEADS * HEAD_DIM
POOL_WIDTH = D_MODEL - ATT_WIDTH
POOL_WINDOWS = (2, 4, 8, 16)
POOL_GROUP = POOL_WIDTH // len(POOL_WINDOWS)
POOL_HALO = 8
POOL_ROWS = 128
N_GROUPS = 3
CLS = 16
BAND = 64
GROUP_WIDTH = 3 * ATT_WIDTH
QKV_WIDTH = N_GROUPS * GROUP_WIDTH
N_BUCKETS = 32
MAX_DISTANCE = 1024
EPS = 1e-6
NEG_INF = -1e30
LANES = 128
QBLK = 128
KWIN = QBLK + 2 * BAND
TOKEN_TILE = 1024
ROW_CHUNK = 512
V7X_VMEM_BYTES = 64 * 1024 * 1024


def _vmem_limit(nbytes):
    return int(min(nbytes, V7X_VMEM_BYTES - 6 * 1024 * 1024))


def _rms_scale(x):
    return lax.rsqrt(jnp.mean(x * x, axis=-1, keepdims=True) + EPS)


def _ffn_kernel(*refs, final):
    if final:
        x_ref, g_ref, wg_ref, wu_ref, wd_ref, fg_ref, o_ref, xn_ref = refs
    else:
        x_ref, g_ref, wg_ref, wu_ref, wd_ref, o_ref, xn_ref = refs
    j = pl.program_id(1)

    def step(first):
        for r in range(x_ref.shape[0] // ROW_CHUNK):
            rows = slice(r * ROW_CHUNK, (r + 1) * ROW_CHUNK)
            if first:
                base = x_ref[rows, :]
                xn = (base * _rms_scale(base) * g_ref[...]).astype(BF16)
                xn_ref[rows, :] = xn
            else:
                base = o_ref[rows, :]
                xn = xn_ref[rows, :]
            a = jnp.dot(xn, wg_ref[...], preferred_element_type=F32)
            b = jnp.dot(xn, wu_ref[...], preferred_element_type=F32)
            h = (a * jax.nn.sigmoid(a)) * (0.5 * b)
            o_ref[rows, :] = base + jnp.dot(h.astype(BF16), wd_ref[...], preferred_element_type=F32)

    pl.when(j == 0)(functools.partial(step, True))
    pl.when(j > 0)(functools.partial(step, False))

    if final:
        @pl.when(j == pl.num_programs(1) - 1)
        def _():
            y = o_ref[...]
            o_ref[...] = y * _rms_scale(y) * fg_ref[...]


def _ffn(x, norm_g, wg, wu, wd, layer, which, norm_idx, final_g=None, *, tm=TOKEN_TILE, tf=512):
    t, d = x.shape
    f = wg.shape[-1]
    final = final_g is not None
    in_specs = [
        pl.BlockSpec((tm, d), lambda i, j: (i, 0)),
        pl.BlockSpec((None, None, 1, d), lambda i, j: (layer, norm_idx, 0, 0)),
        pl.BlockSpec((None, None, d, tf), lambda i, j: (layer, which, 0, j)),
        pl.BlockSpec((None, None, d, tf), lambda i, j: (layer, which, 0, j)),
        pl.BlockSpec((None, None, tf, d), lambda i, j: (layer, which, j, 0)),
    ]
    args = [x, norm_g, wg, wu, wd]
    if final:
        in_specs.append(pl.BlockSpec((1, d), lambda i, j: (0, 0)))
        args.append(final_g)
    vmem = 4 * tm * d * 4 + tm * d * 2 + 2 * 3 * d * tf * 2 + 4 * ROW_CHUNK * tf * 4
    return pl.pallas_call(
        functools.partial(_ffn_kernel, final=final),
        grid=(t // tm, f // tf),
        in_specs=in_specs,
        out_specs=pl.BlockSpec((tm, d), lambda i, j: (i, 0)),
        out_shape=jax.ShapeDtypeStruct((t, d), F32),
        scratch_shapes=[pltpu.VMEM((tm, d), BF16)],
        compiler_params=pltpu.CompilerParams(
            dimension_semantics=("parallel", "arbitrary"),
            vmem_limit_bytes=_vmem_limit(vmem + (8 << 20))),
        name="ffn_final" if final else "ffn",
    )(*args)


def _proj_nat_kernel(x_ref, g_ref, w_ref, wu_ref, z_ref, u_ref, xn_ref):
    def step(first):
        for r in range(x_ref.shape[0] // ROW_CHUNK):
            rows = slice(r * ROW_CHUNK, (r + 1) * ROW_CHUNK)
            if first:
                x = x_ref[rows, :]
                xn = (x * _rms_scale(x) * g_ref[...]).astype(BF16)
                xn_ref[rows, :] = xn
                u_ref[rows, :] = jnp.dot(xn, wu_ref[...], preferred_element_type=F32)
            else:
                xn = xn_ref[rows, :]
            res = jnp.dot(xn, w_ref[...], preferred_element_type=F32)
            for kk in range(z_ref.shape[0]):
                z_ref[kk, rows, :] = res[:, kk * LANES:(kk + 1) * LANES].astype(BF16)

    j = pl.program_id(1)
    pl.when(j == 0)(functools.partial(step, True))
    pl.when(j > 0)(functools.partial(step, False))


def _proj_cls_kernel(x_ref, g_ref, perm_ref, w_ref, z_ref, xn_ref):
    n = ROW_CHUNK // CLS

    def step(first):
        for r in range(x_ref.shape[0] // ROW_CHUNK):
            rows = slice(r * ROW_CHUNK, (r + 1) * ROW_CHUNK)
            if first:
                x = x_ref[rows, :]
                xn = (x * _rms_scale(x) * g_ref[...]).astype(BF16)
                xn = jnp.dot(perm_ref[...], xn, preferred_element_type=F32).astype(BF16)
                xn_ref[rows, :] = xn
            else:
                xn = xn_ref[rows, :]
            res = jnp.dot(xn, w_ref[...], preferred_element_type=F32)
            for kk in range(z_ref.shape[0]):
                for c in range(CLS):
                    z_ref[kk, c, r * n:(r + 1) * n, :] = (
                        res[c * n:(c + 1) * n, kk * LANES:(kk + 1) * LANES].astype(BF16))

    j = pl.program_id(1)
    pl.when(j == 0)(functools.partial(step, True))
    pl.when(j > 0)(functools.partial(step, False))


def _proj_vmem(tm, d, tn):
    return 2 * tm * d * 4 + tm * d * 2 + 2 * d * tn * 2 + 2 * tm * tn * 2 + 2 * ROW_CHUNK * tn * 4


def _proj_nat(x, norm_g, w_in, layer, batch, seq, *, tm=TOKEN_TILE):
    t, d = x.shape
    tn = ATT_WIDTH
    tps = seq // tm
    ucb = QKV_WIDTH // POOL_WIDTH
    return pl.pallas_call(
        _proj_nat_kernel,
        grid=(t // tm, 2),
        in_specs=[
            pl.BlockSpec((tm, d), lambda i, j: (i, 0)),
            pl.BlockSpec((None, None, 1, d), lambda i, j: (layer, 1, 0, 0)),
            pl.BlockSpec((None, d, tn), lambda i, j: (layer, 0, 1 + j)),
            pl.BlockSpec((None, d, POOL_WIDTH), lambda i, j: (layer, 0, ucb)),
        ],
        out_specs=[
            pl.BlockSpec((None, N_HEADS, tm, LANES), lambda i, j: (i // tps, j, i % tps, 0)),
            pl.BlockSpec((tm, POOL_WIDTH), lambda i, j: (i, 0)),
        ],
        out_shape=[
            jax.ShapeDtypeStruct((batch, 2 * N_HEADS, seq, LANES), BF16),
            jax.ShapeDtypeStruct((t, POOL_WIDTH), F32),
        ],
        scratch_shapes=[pltpu.VMEM((tm, d), BF16)],
        compiler_params=pltpu.CompilerParams(
            dimension_semantics=("parallel", "arbitrary"),
            vmem_limit_bytes=_vmem_limit(_proj_vmem(tm, d, tn) + 4 * tm * POOL_WIDTH * 4 + (8 << 20))),
        name="proj_nat",
    )(x, norm_g, w_in, w_in)


def _class_major_perm(tm):
    r = jnp.arange(tm)
    src = (r % (tm // CLS)) * CLS + r // (tm // CLS)
    return (src[:, None] == r[None, :]).astype(BF16)


N_CLS_SLABS = 7


def _proj_cls(x, norm_g, w_in, perm, layer, batch, seq, *, tm=TOKEN_TILE):
    t, d = x.shape
    tn = ATT_WIDTH
    tps = seq // tm
    return pl.pallas_call(
        _proj_cls_kernel,
        grid=(t // tm, N_CLS_SLABS),
        in_specs=[
            pl.BlockSpec((tm, d), lambda i, j: (i, 0)),
            pl.BlockSpec((None, None, 1, d), lambda i, j: (layer, 1, 0, 0)),
            pl.BlockSpec((ROW_CHUNK, ROW_CHUNK), lambda i, j: (0, 0)),
            pl.BlockSpec((None, d, tn), lambda i, j: (layer, 0, jnp.where(j == 0, 0, j + 2))),
        ],
        out_specs=pl.BlockSpec((None, N_HEADS, CLS, tm // CLS, LANES),
                               lambda i, j: (i // tps, j, 0, i % tps, 0)),
        out_shape=jax.ShapeDtypeStruct((batch, N_CLS_SLABS * N_HEADS, CLS, seq // CLS, LANES), BF16),
        scratch_shapes=[pltpu.VMEM((tm, d), BF16)],
        compiler_params=pltpu.CompilerParams(
            dimension_semantics=("parallel", "arbitrary"),
            vmem_limit_bytes=_vmem_limit(_proj_vmem(tm, d, tn) + 2 * ROW_CHUNK * ROW_CHUNK * 2 + (8 << 20))),
        name="proj_cls",
    )(x, norm_g, perm, w_in)


ATTN_UNROLL = (16, 4, 8)


def _scores(q, k, bias):
    s = lax.dot_general(q, k, (((1,), (1,)), ((), ())), preferred_element_type=F32)
    return s * (HEAD_DIM ** -0.5) + bias


def _softmax(s):
    m = jnp.max(s, axis=-1, keepdims=True)
    p = jnp.exp(s - m)
    l = jnp.sum(p, axis=-1, keepdims=True)
    return p.astype(BF16), m, l


def _attn_kernel(q0c, k0, v0, q1, k1, v1, q2, k2, v2, b01_ref, b2_ref, unperm_ref, o_ref,
                 osc, lsc, q0f, *, seq):
    u2, u1, u0 = ATTN_UNROLL
    cls_len = seq // CLS
    sub = QBLK // 4
    ksub = KWIN // 4
    rpc = QBLK // CLS

    def g2_body(it, carry):
        cs = [it * u2 + u for u in range(u2)]
        ss = [_scores(q2[c], k2[c], b2_ref[...]) for c in cs]
        pml = [_softmax(s) for s in ss]
        outs = [jnp.dot(p, v2[c], preferred_element_type=F32) for (p, _, _), c in zip(pml, cs)]
        for c, o, (_, m, l) in zip(cs, outs, pml):
            osc[1, c] = o * (1.0 / l)
            lsc[1, c] = jnp.broadcast_to(m + jnp.log(l), (QBLK, LANES))
            q0f[c] = q0c[c].astype(F32)
        return carry

    lax.fori_loop(0, CLS // u2, g2_body, 0)

    nblk1 = (seq // 4) // QBLK
    kss1 = [min(max(sub * i - BAND // 4, 0), cls_len - ksub) for i in range(nblk1)]
    var1 = [0 if i == 0 else (2 if i == nblk1 - 1 else 1) for i in range(nblk1)]

    def g1_body(it, carry):
        blocks = [(it * u1 + uu, i) for uu in range(u1) for i in range(nblk1)]
        ss = []
        for c4, i in blocks:
            q = jnp.concatenate([q1[4 * a + c4, sub * i:sub * (i + 1), :] for a in range(4)], axis=0)
            k = jnp.concatenate([k1[4 * a + c4, kss1[i]:kss1[i] + ksub, :] for a in range(4)], axis=0)
            ss.append(_scores(q, k, b01_ref[1, var1[i]]))
        pml = [_softmax(s) for s in ss]
        outs = []
        for (c4, i), (p, _, _) in zip(blocks, pml):
            v = jnp.concatenate([v1[4 * a + c4, kss1[i]:kss1[i] + ksub, :] for a in range(4)], axis=0)
            outs.append(jnp.dot(p, v, preferred_element_type=F32))
        for (c4, i), o, (_, m, l) in zip(blocks, outs, pml):
            o = o * (1.0 / l)
            lse = jnp.broadcast_to(m + jnp.log(l), (QBLK, LANES))
            for a in range(4):
                osc[0, 4 * a + c4, sub * i:sub * (i + 1), :] = o[sub * a:sub * (a + 1)]
                lsc[0, 4 * a + c4, sub * i:sub * (i + 1), :] = lse[sub * a:sub * (a + 1)]
        return carry

    lax.fori_loop(0, 4 // u1, g1_body, 0)

    nblk0 = seq // QBLK

    def class_rows(ref, lead, r0):
        return jnp.concatenate([ref[lead + (c, pl.ds(r0, rpc), slice(None))] for c in range(CLS)], axis=0)

    def g0_body(it, carry):
        idx = [it * u0 + u for u in range(u0)]
        r0s = [pl.multiple_of(i * rpc, rpc) for i in idx]
        kss = [pl.multiple_of(jnp.clip(i * QBLK - BAND, 0, seq - KWIN), BAND) for i in idx]
        var = [jnp.where(i == 0, 0, jnp.where(i == nblk0 - 1, 2, 1)) for i in idx]
        ss = [_scores(class_rows(q0f, (), r0).astype(BF16), k0[pl.ds(ks, KWIN), :], b01_ref[0, vr])
              for r0, ks, vr in zip(r0s, kss, var)]
        pml = [_softmax(s) for s in ss]
        outs = [jnp.dot(p, v0[pl.ds(ks, KWIN), :], preferred_element_type=F32)
                for (p, _, _), ks in zip(pml, kss)]
        atts = []
        for r0, o0, (_, m0, l0) in zip(r0s, outs, pml):
            lse1 = class_rows(lsc, (0,), r0)
            lse2 = class_rows(lsc, (1,), r0)
            mx = jnp.maximum(jnp.maximum(lse1, lse2), m0)
            w0 = jnp.exp(m0 - mx)
            w1 = jnp.exp(lse1 - mx)
            w2 = jnp.exp(lse2 - mx)
            num = w0 * o0 + w1 * class_rows(osc, (0,), r0) + w2 * class_rows(osc, (1,), r0)
            den = w0 * l0 + w1 + w2
            atts.append((num / den).astype(BF16))
        for i, att in zip(idx, atts):
            nat = jnp.dot(unperm_ref[...], att, preferred_element_type=F32)
            o_ref[pl.ds(pl.multiple_of(i * QBLK, QBLK), QBLK), :] = nat.astype(BF16)
        return carry

    lax.fori_loop(0, nblk0 // u0, g0_body, 0)


def _attention(z0kv, zc, b01, b2, unperm, batch, seq):
    def cls_spec(slab):
        return pl.BlockSpec((None, None, CLS, seq // CLS, LANES),
                            lambda h, b: (b, slab * N_HEADS + h, 0, 0, 0))

    def nat_spec(t):
        return pl.BlockSpec((None, None, seq, LANES), lambda h, b: (b, t * N_HEADS + h, 0, 0))

    in_specs = [cls_spec(0), nat_spec(0), nat_spec(1)] + [cls_spec(slab) for slab in range(1, N_CLS_SLABS)]
    in_specs += [
        pl.BlockSpec((None, 2, 3, QBLK, KWIN), lambda h, b: (h, 0, 0, 0, 0)),
        pl.BlockSpec((None, QBLK, QBLK), lambda h, b: (h, 0, 0)),
        pl.BlockSpec((QBLK, QBLK), lambda h, b: (0, 0)),
    ]
    args = [zc, z0kv, z0kv] + [zc] * (N_CLS_SLABS - 1) + [b01, b2, unperm]
    vmem = (2 * 9 * seq * LANES * 2 + 2 * 8 * QBLK * KWIN * 4 + 2 * seq * LANES * 2
            + 5 * seq * LANES * 4)
    return pl.pallas_call(
        functools.partial(_attn_kernel, seq=seq),
        grid=(N_HEADS, batch),
        in_specs=in_specs,
        out_specs=pl.BlockSpec((None, None, seq, LANES), lambda h, b: (b, h, 0, 0)),
        out_shape=jax.ShapeDtypeStruct((batch, N_HEADS, seq, LANES), BF16),
        scratch_shapes=[pltpu.VMEM((2, CLS, seq // CLS, LANES), F32),
                        pltpu.VMEM((2, CLS, seq // CLS, LANES), F32),
                        pltpu.VMEM((CLS, seq // CLS, LANES), F32)],
        compiler_params=pltpu.CompilerParams(
            dimension_semantics=("parallel", "arbitrary"),
            vmem_limit_bytes=_vmem_limit(vmem + (16 << 20))),
        name="attn",
    )(*args)


def _mix_out_kernel(x_ref, att_ref, up_ref, u_ref, un_ref, wp_ref, ps_ref, wo_ref, o_ref, ext_ref,
                    *, seq):
    i = pl.program_id(0)
    tm = x_ref.shape[0]
    pos0 = (i % (seq // tm)) * tm
    ext_ref[0:POOL_HALO, :] = jnp.where(pos0 > 0, up_ref[...], 0.0)
    ext_ref[POOL_HALO:POOL_HALO + tm, :] = u_ref[...]
    ext_ref[POOL_HALO + tm:2 * POOL_HALO + tm, :] = jnp.where(pos0 + tm < seq, un_ref[...], 0.0)
    iota = lax.broadcasted_iota(jnp.int32, (POOL_ROWS, 1), 0)

    att = jnp.concatenate([att_ref[h] for h in range(N_HEADS)], axis=-1)
    acc = x_ref[...] + jnp.dot(att, wo_ref[0:ATT_WIDTH, :], preferred_element_type=F32)
    pooled = []
    for g, w in enumerate(POOL_WINDOWS):
        h = w // 2
        cols = slice(g * POOL_GROUP, (g + 1) * POOL_GROUP)
        dgs = []
        for r0 in range(0, tm, POOL_ROWS):
            base = POOL_HALO + r0
            ssum = ext_ref[base - h:base - h + POOL_ROWS, cols]
            for kk in range(-h + 1, h + 1):
                ssum = ssum + ext_ref[base + kk:base + kk + POOL_ROWS, cols]
            pos = pos0 + r0 + iota
            cnt = (jnp.minimum(pos + h + 1, seq) - jnp.maximum(pos - h, 0)).astype(F32)
            dgs.append((ssum / cnt - u_ref[r0:r0 + POOL_ROWS, cols]).astype(BF16))
        y = jnp.dot(jnp.concatenate(dgs, axis=0), wp_ref[g], preferred_element_type=F32)
        pooled.append((y * ps_ref[:, cols]).astype(BF16))
    pooled = jnp.concatenate(pooled, axis=-1)
    o_ref[...] = acc + jnp.dot(pooled, wo_ref[ATT_WIDTH:, :], preferred_element_type=F32)


def _mix_out(x, att, u, w_pool, pool_scale, w_out, layer, seq, *, tm=ROW_CHUNK):
    t, d = x.shape
    tps = seq // tm
    hb = tm // POOL_HALO
    nhb = t // POOL_HALO
    vmem = 4 * tm * d * 4 + 2 * tm * ATT_WIDTH * 2 + 3 * tm * POOL_WIDTH * 4 + 2 * d * d * 2 + 2 * tm * d * 4
    return pl.pallas_call(
        functools.partial(_mix_out_kernel, seq=seq),
        grid=(t // tm,),
        in_specs=[
            pl.BlockSpec((tm, d), lambda i: (i, 0)),
            pl.BlockSpec((None, N_HEADS, tm, LANES), lambda i: (i // tps, 0, i % tps, 0)),
            pl.BlockSpec((POOL_HALO, POOL_WIDTH), lambda i: (jnp.maximum(i * hb - 1, 0), 0)),
            pl.BlockSpec((tm, POOL_WIDTH), lambda i: (i, 0)),
            pl.BlockSpec((POOL_HALO, POOL_WIDTH), lambda i: (jnp.minimum((i + 1) * hb, nhb - 1), 0)),
            pl.BlockSpec((None, len(POOL_WINDOWS), POOL_GROUP, POOL_GROUP), lambda i: (layer, 0, 0, 0)),
            pl.BlockSpec((None, 1, POOL_WIDTH), lambda i: (layer, 0, 0)),
            pl.BlockSpec((None, d, d), lambda i: (layer, 0, 0)),
        ],
        out_specs=pl.BlockSpec((tm, d), lambda i: (i, 0)),
        out_shape=jax.ShapeDtypeStruct((t, d), F32),
        scratch_shapes=[pltpu.VMEM((tm + 2 * POOL_HALO, POOL_WIDTH), F32)],
        compiler_params=pltpu.CompilerParams(
            dimension_semantics=("parallel",),
            vmem_limit_bytes=_vmem_limit(vmem + (8 << 20))),
        name="mix_out",
    )(x, att, u, u, u, w_pool, pool_scale, w_out)


def _t5_bucket(rel):
    half = N_BUCKETS // 2
    max_exact = half // 2
    ret = jnp.where(rel > 0, half, 0)
    n = jnp.abs(rel)
    nf = jnp.maximum(n, 1).astype(F32)
    large = max_exact + (jnp.log(nf / max_exact) / math.log(MAX_DISTANCE / max_exact)
                         * (half - max_exact)).astype(jnp.int32)
    large = jnp.minimum(large, half - 1)
    return ret + jnp.where(n < max_exact, n, large)


def _bias_lookup(table, delta, dil):
    onehot = (_t5_bucket(delta * dil)[..., None] == jnp.arange(N_BUCKETS)).astype(F32)
    b = jnp.einsum("...k,kh->h...", onehot, table, precision=lax.Precision.HIGHEST)
    return jnp.where(jnp.abs(delta) <= BAND, b, NEG_INF)


def _block_class_positions():
    r = jnp.arange(QBLK)
    rpc = QBLK // CLS
    return CLS * (r % rpc) + r // rpc


def _bias_tiles(rel_bias):
    rb = rel_bias.astype(F32)
    qi = jnp.arange(QBLK)[:, None]
    kj = jnp.arange(KWIN)[None, :]
    q0 = _block_class_positions()[:, None]
    q1 = 4 * (qi % (QBLK // 4)) + qi // (QBLK // 4)
    k1 = 4 * (kj % (KWIN // 4)) + kj // (KWIN // 4)
    g0 = jnp.stack([_bias_lookup(rb[:, 0:N_HEADS], kj - q0 - v * BAND, 1) for v in range(3)], axis=1)
    g1 = jnp.stack([_bias_lookup(rb[:, N_HEADS:2 * N_HEADS], k1 - q1 - v * BAND, 4)
                    for v in range(3)], axis=1)
    b2 = _bias_lookup(rb[:, 2 * N_HEADS:3 * N_HEADS], jnp.arange(QBLK)[None, :] - qi, 16)
    return jnp.stack([g0, g1], axis=1), b2


def _block_unperm():
    return (_block_class_positions()[None, :] == jnp.arange(QBLK)[:, None]).astype(BF16)


def _trunk(x3, p):
    batch, seq, d = x3.shape
    x = x3.reshape(batch * seq, d)
    depth = p["w_in"].shape[0]
    for l in range(depth):
        x = _ffn(x, p["norm_g"], p["ffn_gate"], p["ffn_up"], p["ffn_down"], l, 0, 0)
        z0kv, u = _proj_nat(x, p["norm_g"], p["w_in"], l, batch, seq)
        zc = _proj_cls(x, p["norm_g"], p["w_in"], p["perm"], l, batch, seq)
        att = _attention(z0kv, zc, p["b01"], p["b2"], p["unperm"], batch, seq)
        x = _mix_out(x, att, u, p["w_pool"], p["pool_scale"], p["w_out"], l, seq)
        x = _ffn(x, p["norm_g"], p["ffn_gate"], p["ffn_up"], p["ffn_down"], l, 1, 2,
                 p["final_g"] if l == depth - 1 else None)
    return x.reshape(batch, seq, d)


def kernel(x_prompt, x_sample, norm_g, ffn_gate, ffn_up, ffn_down, w_in, w_pool, pool_scale,
           w_out, rel_bias, final_g):
    depth = w_in.shape[0]
    b01, b2 = _bias_tiles(rel_bias)
    p = {
        "norm_g": norm_g.reshape(depth, 3, 1, D_MODEL),
        "ffn_gate": ffn_gate.astype(BF16),
        "ffn_up": ffn_up.astype(BF16),
        "ffn_down": ffn_down.astype(BF16),
        "w_in": w_in.astype(BF16),
        "w_pool": w_pool.astype(BF16),
        "pool_scale": pool_scale.reshape(depth, 1, POOL_WIDTH),
        "w_out": w_out.astype(BF16),
        "b01": b01,
        "b2": b2,
        "perm": _class_major_perm(ROW_CHUNK),
        "unperm": _block_unperm(),
        "final_g": final_g.reshape(1, D_MODEL),
    }
    return (_trunk(x_prompt, p), _trunk(x_sample, p))
```
